```python
import math
import jax, jax.numpy as jnp
from jax import lax
import numpy as np

D_MODEL = 1024
BATCH = 4
SEQ = 4096
DEPTH = 4
DEC_BATCH = 128
DEC_SEQ = 8
PAST_LEN = 2048
PAGE_SIZE = 128

LRU_WIDTH = D_MODEL // 2
LRU_BLOCKS = 8
LRU_BLOCK_W = LRU_WIDTH // LRU_BLOCKS
CONV_W = 4
LRU_C = 8.0
N_HEADS = 8
HEAD_DIM = (D_MODEL // 2) // N_HEADS
ATTN_WIDTH = N_HEADS * HEAD_DIM
MIX_WIDTH = LRU_WIDTH + ATTN_WIDTH
IDX_HEADS = 4
IDX_DIM = 64
TOPK_MAX = 256
ROPE_THETA = 500000.0
ROPE_FRAC_DIV = 4
Q_BLOCK = 128
PEER_HEADS = 8
PEER_NKEYS = 128
PEER_EXPERTS = PEER_NKEYS * PEER_NKEYS
PEER_DKEY = 256
PEER_TOPK = 16
PEER_CHUNK = 128
ALPHA = (2 * DEPTH) ** 0.25
BETA = (8 * DEPTH) ** -0.25
LN_EPS = 1e-5
IN_SPLITS = (LRU_WIDTH, 2 * LRU_WIDTH, 2 * LRU_WIDTH + ATTN_WIDTH, 2 * LRU_WIDTH + 2 * ATTN_WIDTH,
             2 * LRU_WIDTH + 3 * ATTN_WIDTH, 2 * LRU_WIDTH + 3 * ATTN_WIDTH + IDX_HEADS * IDX_DIM,
             2 * LRU_WIDTH + 3 * ATTN_WIDTH + IDX_HEADS * IDX_DIM + IDX_DIM)
IN_COLS = IN_SPLITS[-1] + IDX_HEADS

kernel_name = 'hymba_rglru_dsa_peer_deepnorm_step'


def layer_norm(x, g, b):
    xf = x.astype(jnp.float32)
    mu = jnp.mean(xf, -1, keepdims=True)
    var = jnp.mean(jnp.square(xf - mu), -1, keepdims=True)
    return ((xf - mu) * lax.rsqrt(var + LN_EPS) * g + b).astype(x.dtype)


def rms_scale(x, g):
    xf = x.astype(jnp.float32)
    return (xf * lax.rsqrt(jnp.mean(xf * xf, -1, keepdims=True) + LN_EPS) * g).astype(x.dtype)


def partial_rope(x, pos):
    rot = x.shape[-1] // ROPE_FRAC_DIV
    half = rot // 2
    freqs = ROPE_THETA ** (-jnp.arange(half, dtype=jnp.float32) / half)
    ang = pos.astype(jnp.float32)[:, None] * freqs[None, :]
    cos = jnp.cos(ang)[:, None, :]
    sin = jnp.sin(ang)[:, None, :]
    xr = x[..., :rot].astype(jnp.float32)
    x1, x2 = xr[..., :half], xr[..., half:]
    out = jnp.concatenate([x1 * cos - x2 * sin, x2 * cos + x1 * sin], -1)
    return jnp.concatenate([out.astype(x.dtype), x[..., rot:]], -1)


def causal_conv(x, buf, w, b):
    xp = jnp.concatenate([buf, x], 1)
    T = x.shape[1]
    y = b + sum(xp[:, k:k + T] * w[k] for k in range(CONV_W))
    return y, xp[:, -(CONV_W - 1):]


def rg_lru(xb, h0, wa, ba, wx, bx, lam):
    B, T, W = xb.shape
    xblk = xb.reshape(B, T, LRU_BLOCKS, LRU_BLOCK_W)
    r = jax.nn.sigmoid(jnp.einsum('btnc,ncd->btnd', xblk, wa).reshape(B, T, W) + ba)
    i = jax.nn.sigmoid(jnp.einsum('btnc,ncd->btnd', xblk, wx).reshape(B, T, W) + bx)
    log_a = -LRU_C * r.astype(jnp.float32) * jax.nn.softplus(-lam.astype(jnp.float32))
    a = jnp.exp(log_a)
    u = jnp.sqrt(-jnp.expm1(2.0 * log_a)) * (i * xb).astype(jnp.float32)

    def step(h, au):
        a_t, u_t = au
        h = a_t * h + u_t
        return h, h

    hT, hs = lax.scan(step, h0.astype(jnp.float32), (a.swapaxes(0, 1), u.swapaxes(0, 1)))
    return hs.swapaxes(0, 1).astype(xb.dtype), hT.astype(h0.dtype)


def gather_rows(src, idx):
    return jax.vmap(lambda s, i: s[i])(src, idx)


def dsa_select(qi, wi, ki, q_pos, topk):
    logits = jnp.einsum('bqhd,bsd->bqhs', qi, ki).astype(jnp.float32)
    score = jnp.einsum('bqh,bqhs->bqs', wi.astype(jnp.float32), jax.nn.relu(logits))
    k_pos = jnp.arange(ki.shape[1], dtype=jnp.int32)
    visible = k_pos[None, :] <= q_pos[:, None]
    score = jnp.where(visible[None], score, -jnp.inf)
    _, idx = lax.top_k(score, topk)
    valid = idx <= q_pos[None, :, None]
    return idx, valid


def sparse_attend(q, k_sel, v_sel, valid):
    s = jnp.einsum('bqhd,bqkhd->bqhk', q, k_sel).astype(jnp.float32) * (HEAD_DIM ** -0.5)
    s = jnp.where(valid[:, :, None, :], s, -jnp.inf)
    p = jax.nn.softmax(s, -1)
    return jnp.einsum('bqhk,bqkhd->bqhd', p.astype(v_sel.dtype), v_sel)


def dsa_prompt(q, k, v, qi, ki, wi, pos):
    B, T = q.shape[:2]
    topk = min(TOPK_MAX, T // 4)
    nb = T // Q_BLOCK

    def split(a):
        return a.reshape(B, nb, Q_BLOCK, *a.shape[2:]).swapaxes(0, 1)

    def block(args):
        qb, qib, wib, pb = args
        idx, valid = dsa_select(qib, wib, ki, pb, topk)
        return sparse_attend(qb, gather_rows(k, idx), gather_rows(v, idx), valid)

    out = lax.map(block, (split(q), split(qi), split(wi), pos.reshape(nb, Q_BLOCK)))
    return out.swapaxes(0, 1).reshape(B, T, N_HEADS, HEAD_DIM)


def dsa_sample(q, k_new, v_new, qi, ki_new, wi, pos, pool_k, pool_v, pool_ki, page_table):
    B, T = q.shape[:2]
    past = page_table.shape[1] * PAGE_SIZE
    topk = min(TOPK_MAX, (past + T) // 4)
    ki_past = pool_ki[page_table].reshape(B, past, IDX_DIM)
    idx, valid = dsa_select(qi, wi, jnp.concatenate([ki_past, ki_new], 1), pos, topk)
    in_past = (idx < past)[..., None, None]
    pidx = jnp.minimum(idx, past - 1)
    phys = gather_rows(page_table, pidx // PAGE_SIZE)
    row = phys * PAGE_SIZE + pidx % PAGE_SIZE
    nidx = jnp.clip(idx - past, 0, T - 1)
    flat_k = pool_k.reshape(-1, N_HEADS, HEAD_DIM)
    flat_v = pool_v.reshape(-1, N_HEADS, HEAD_DIM)
    k_sel = jnp.where(in_past, flat_k[row], gather_rows(k_new, nidx))
    v_sel = jnp.where(in_past, flat_v[row], gather_rows(v_new, nidx))
    return sparse_attend(q, k_sel, v_sel, valid)


def peer(x, wq, sub_keys, u_tab, v_tab):
    B, T, D = x.shape
    n = B * T
    pad = (-n) % PEER_CHUNK
    xt = jnp.pad(x.reshape(n, D), ((0, pad), (0, 0)))
    chunks = xt.reshape(-1, PEER_CHUNK, D)

    def chunk_fn(xc):
        qh = (xc @ wq).reshape(PEER_CHUNK, PEER_HEADS, 2, PEER_DKEY // 2)
        s = jnp.einsum('chpd,phkd->chpk', qh, sub_keys).astype(jnp.float32)
        sv, si = lax.top_k(s, PEER_TOPK)
        cand = sv[:, :, 0, :, None] + sv[:, :, 1, None, :]
        cand_idx = si[:, :, 0, :, None] * PEER_NKEYS + si[:, :, 1, None, :]
        fv, fi = lax.top_k(cand.reshape(PEER_CHUNK, PEER_HEADS, -1), PEER_TOPK)
        experts = jnp.take_along_axis(cand_idx.reshape(PEER_CHUNK, PEER_HEADS, -1), fi, -1)
        g = jax.nn.softmax(fv, -1)
        act = jax.nn.gelu(jnp.einsum('cd,chkd->chk', xc, u_tab[experts]).astype(jnp.float32), approximate=False)
        return jnp.einsum('chk,chkd->cd', (g * act).astype(xc.dtype), v_tab[experts])

    out = lax.map(chunk_fn, chunks).reshape(-1, D)[:n]
    return out.reshape(B, T, D)


def trunk_layer(x, c, pos, conv_buf, h0, attend, w_ada, b_ada, w_in, conv_w, conv_b, lru_wa, lru_ba,
                lru_wx, lru_bx, lru_lambda, beta_lru, beta_attn, w_out, ln1_g, ln1_b, ln2_g, ln2_b,
                peer_wq, peer_keys, peer_u, peer_v):
    B, T, _ = x.shape
    mod = jax.nn.silu(c) @ w_ada + b_ada
    sh1, sc1, g1, sh2, sc2, g2 = jnp.split(mod, 6, -1)
    h = x * (1 + sc1[:, None]) + sh1[:, None]
    z = h @ w_in
    zx, zg, zq, zk, zv, zqi, zki, zwi = jnp.split(z, IN_SPLITS, -1)
    xc, new_conv = causal_conv(zx, conv_buf, conv_w, conv_b)
    hs, hT = rg_lru(xc, h0, lru_wa, lru_ba, lru_wx, lru_bx, lru_lambda)
    y_lru = jax.nn.gelu(zg, approximate=False) * hs
    q = partial_rope(zq.reshape(B, T, N_HEADS, HEAD_DIM), pos)
    k = partial_rope(zk.reshape(B, T, N_HEADS, HEAD_DIM), pos)
    v = zv.reshape(B, T, N_HEADS, HEAD_DIM)
    qi = partial_rope(zqi.reshape(B, T, IDX_HEADS, IDX_DIM), pos)
    ki = partial_rope(zki[:, :, None, :], pos)[:, :, 0]
    wi = zwi * (IDX_HEADS ** -0.5)
    o = attend(q, k, v, qi, ki, wi).reshape(B, T, ATTN_WIDTH)
    mix = jnp.concatenate([rms_scale(y_lru, beta_lru), rms_scale(o, beta_attn)], -1)
    x = layer_norm(ALPHA * x + g1[:, None] * (mix @ w_out), ln1_g, ln1_b)
    h2 = x * (1 + sc2[:, None]) + sh2[:, None]
    x = layer_norm(ALPHA * x + g2[:, None] * peer(h2, peer_wq, peer_keys, peer_u, peer_v), ln2_g, ln2_b)
    return x, new_conv, hT, k, v, ki


def setup_inputs(seed: int = 0) -> dict:
    key = jax.random.key(seed)
    ks = jax.random.split(key, 40)
    f32 = jnp.float32

    def nrm(k, shape, s):
        return jax.random.normal(k, shape, f32) * s

    n_pages = PAST_LEN // PAGE_SIZE
    n_used = DEC_BATCH * n_pages
    n_pool = n_used + max(1, n_used // 4)
    page_table = jax.random.permutation(ks[0], n_pool)[:n_used].reshape(DEC_BATCH, n_pages).astype(jnp.int32)
    a_c = jax.random.uniform(ks[1], (DEPTH, LRU_WIDTH), f32, 0.9, 0.999)
    a = a_c ** (1.0 / LRU_C)
    lru_lambda = jnp.log(a) - jnp.log1p(-a)
    return {
        'x_prompt': nrm(ks[2], (BATCH, SEQ, D_MODEL), 1.0),
        'x_sample': nrm(ks[3], (DEC_BATCH, DEC_SEQ, D_MODEL), 1.0),
        'cache_k': nrm(ks[4], (DEPTH, n_pool, PAGE_SIZE, N_HEADS, HEAD_DIM), 1.0),
        'cache_v': nrm(ks[5], (DEPTH, n_pool, PAGE_SIZE, N_HEADS, HEAD_DIM), 1.0),
        'cache_idx_k': nrm(ks[6], (DEPTH, n_pool, PAGE_SIZE, IDX_DIM), 1.0),
        'state_conv': nrm(ks[7], (DEPTH, DEC_BATCH, CONV_W - 1, LRU_WIDTH), 1.0),
        'state_lru': nrm(ks[8], (DEPTH, DEC_BATCH, LRU_WIDTH), 0.5),
        'page_table': page_table,
        'c_prompt': nrm(ks[9], (BATCH, D_MODEL), 1.0),
        'c_sample': nrm(ks[10], (DEC_BATCH, D_MODEL), 1.0),
        'w_ada': nrm(ks[11], (DEPTH, D_MODEL, 6 * D_MODEL), 0.5 * D_MODEL ** -0.5),
        'b_ada': nrm(ks[12], (DEPTH, 6 * D_MODEL), 0.01),
        'w_in': nrm(ks[13], (DEPTH, D_MODEL, IN_COLS), D_MODEL ** -0.5),
        'conv_w': nrm(ks[14], (DEPTH, CONV_W, LRU_WIDTH), CONV_W ** -0.5),
        'conv_b': nrm(ks[15], (DEPTH, LRU_WIDTH), 0.01),
        'lru_wa': nrm(ks[16], (DEPTH, LRU_BLOCKS, LRU_BLOCK_W, LRU_BLOCK_W), LRU_BLOCK_W ** -0.5),
        'lru_ba': nrm(ks[17], (DEPTH, LRU_WIDTH), 0.01),
        'lru_wx': nrm(ks[18], (DEPTH, LRU_BLOCKS, LRU_BLOCK_W, LRU_BLOCK_W), LRU_BLOCK_W ** -0.5),
        'lru_bx': nrm(ks[19], (DEPTH, LRU_WIDTH), 0.01),
        'lru_lambda': lru_lambda,
        'beta_lru': 1.0 + nrm(ks[20], (DEPTH, LRU_WIDTH), 0.01),
        'beta_attn': 1.0 + nrm(ks[21], (DEPTH, ATTN_WIDTH), 0.01),
        'w_out': nrm(ks[22], (DEPTH, MIX_WIDTH, D_MODEL), BETA * MIX_WIDTH ** -0.5),
        'ln1_g': 1.0 + nrm(ks[23], (DEPTH, D_MODEL), 0.01),
        'ln1_b': nrm(ks[24], (DEPTH, D_MODEL), 0.01),
        'ln2_g': 1.0 + nrm(ks[25], (DEPTH, D_MODEL), 0.01),
        'ln2_b': nrm(ks[26], (DEPTH, D_MODEL), 0.01),
        'peer_wq': nrm(ks[27], (DEPTH, D_MODEL, PEER_HEADS * PEER_DKEY), D_MODEL ** -0.5),
        'peer_keys': nrm(ks[28], (DEPTH, 2, PEER_HEADS, PEER_NKEYS, PEER_DKEY // 2), (PEER_DKEY // 2) ** -0.5),
        'peer_u': nrm(ks[29], (DEPTH, PEER_EXPERTS, D_MODEL), D_MODEL ** -0.5),
        'peer_v': nrm(ks[30], (DEPTH, PEER_EXPERTS, D_MODEL), BETA),
    }


def reference(x_prompt, x_sample, cache_k, cache_v, cache_idx_k, state_conv, state_lru, page_table,
              c_prompt, c_sample, w_ada, b_ada, w_in, conv_w, conv_b, lru_wa, lru_ba, lru_wx, lru_bx,
              lru_lambda, beta_lru, beta_attn, w_out, ln1_g, ln1_b, ln2_g, ln2_b, peer_wq, peer_keys,
              peer_u, peer_v):
    b_p, t_p = x_prompt.shape[:2]
    past = page_table.shape[1] * PAGE_SIZE
    pos_p = jnp.arange(t_p, dtype=jnp.int32)
    pos_s = past + jnp.arange(x_sample.shape[1], dtype=jnp.int32)
    yp, ys = x_prompt, x_sample
    kp_l, vp_l, kip_l, cp_l, hp_l = [], [], [], [], []
    ks_l, vs_l, kis_l, cs_l, hs_l = [], [], [], [], []
    for l in range(DEPTH):
        wl = (w_ada[l], b_ada[l], w_in[l], conv_w[l], conv_b[l], lru_wa[l], lru_ba[l], lru_wx[l], lru_bx[l],
              lru_lambda[l], beta_lru[l], beta_attn[l], w_out[l], ln1_g[l], ln1_b[l], ln2_g[l], ln2_b[l],
              peer_wq[l], peer_keys[l], peer_u[l], peer_v[l])
        attend_p = lambda q, k, v, qi, ki, wi: dsa_prompt(q, k, v, qi, ki, wi, pos_p)
        attend_s = lambda q, k, v, qi, ki, wi, l=l: dsa_sample(q, k, v, qi, ki, wi, pos_s, cache_k[l], cache_v[l],
                                                             cache_idx_k[l], page_table)
        zc = jnp.zeros((b_p, CONV_W - 1, LRU_WIDTH), x_prompt.dtype)
        zh = jnp.zeros((b_p, LRU_WIDTH), x_prompt.dtype)
        yp, cp, hp, kp, vp, kip = trunk_layer(yp, c_prompt, pos_p, zc, zh, attend_p, *wl)
        ys, cs, hs, k_s, v_s, kis = trunk_layer(ys, c_sample, pos_s, state_conv[l], state_lru[l], attend_s, *wl)
        kp_l.append(kp); vp_l.append(vp); kip_l.append(kip); cp_l.append(cp); hp_l.append(hp)
        ks_l.append(k_s); vs_l.append(v_s); kis_l.append(kis); cs_l.append(cs); hs_l.append(hs)
    new_k_prompt = jnp.stack(kp_l)
    new_v_prompt = jnp.stack(vp_l)
    new_idx_k_prompt = jnp.stack(kip_l)
    new_conv_prompt = jnp.stack(cp_l)
    new_lru_prompt = jnp.stack(hp_l)
    new_k_sample = jnp.stack(ks_l)
    new_v_sample = jnp.stack(vs_l)
    new_idx_k_sample = jnp.stack(kis_l)
    new_conv_sample = jnp.stack(cs_l)
    new_lru_sample = jnp.stack(hs_l)
    return (yp, ys, new_k_prompt, new_v_prompt, new_idx_k_prompt, new_conv_prompt, new_lru_prompt,
            new_k_sample, new_v_sample, new_idx_k_sample, new_conv_sample, new_lru_sample)
```

```python
import functools
import math

import jax
import jax.numpy as jnp
from jax import lax
from jax.experimental import pallas as pl
from jax.experimental.pallas import tpu as pltpu

F32 = jnp.float32
BF16 = jnp.bfloat16
I32 = jnp.int32

LRU_C = 8.0
CONV_W = 4
N_HEADS = 8
HEAD_DIM = 64
IDX_HEADS = 4
IDX_DIM = 64
TOPK_MAX = 256
ROPE_THETA = 500000.0
ROPE_FRAC_DIV = 4
Q_BLOCK = 128
PEER_HEADS = 8
PEER_TOPK = 16
DEPTH_NOMINAL = 4
ALPHA = (2 * DEPTH_NOMINAL) ** 0.25
LN_EPS = 1e-5

LANE = 128
VMEM_LIMIT = 56 * 1024 * 1024

NN = (((1,), (0,)), ((), ()))
NT = (((1,), (1,)), ((), ()))
TN = (((0,), (0,)), ((), ()))

NEG_INF = float("-inf")
INT_MIN = -2147483648
KEY_NEG_INF = -2139095041


def _cparams(*sem):
    return pltpu.CompilerParams(dimension_semantics=sem, vmem_limit_bytes=VMEM_LIMIT)


def _split(a):
    hi = a.astype(BF16)
    lo = (a - hi.astype(F32)).astype(BF16)
    return hi, lo


def _dot(a, b, dims=NN):
    return lax.dot_general(a, b, dims, preferred_element_type=F32)


def _dot3s(ah, al, bh, bl, dims=NN):
    return _dot(ah, bh, dims) + (_dot(ah, bl, dims) + _dot(al, bh, dims))


def _dot3w(a, bh, bl, dims=NN):
    ah, al = _split(a)
    return _dot3s(ah, al, bh, bl, dims)


def _gelu(x):
    return 0.5 * x * (1.0 + lax.erf(x * (1.0 / math.sqrt(2.0))))


def _layer_norm(y, g, b):
    mu = jnp.mean(y, -1, keepdims=True)
    d = y - mu
    var = jnp.mean(d * d, -1, keepdims=True)
    return d * lax.rsqrt(var + LN_EPS) * g + b


def _rms(y, g):
    return y * lax.rsqrt(jnp.mean(y * y, -1, keepdims=True) + LN_EPS) * g


def _order_key(x):
    b = pltpu.bitcast(x, I32)
    return jnp.where(b < 0, b ^ 0x7FFFFFFF, b)


def _ada_kernel(c_ref, w_ref, b_ref, o_ref):
    c = c_ref[...]
    s = c * jax.nn.sigmoid(c)
    wh, wl = _split(w_ref[...])
    o_ref[...] = _dot3w(s, wh, wl) + b_ref[...]


def _ada_mod(c_all, w_ada, b_ada):
    depth, d, six_d = w_ada.shape
    nb = c_all.shape[0]
    return pl.pallas_call(
        _ada_kernel,
        grid=(depth, six_d // d),
        in_specs=[
            pl.BlockSpec((nb, d), lambda l, j: (0, 0)),
            pl.BlockSpec((None, d, d), lambda l, j: (l, 0, j)),
            pl.BlockSpec((None, None, 1, d), lambda l, j: (l, j, 0, 0)),
        ],
        out_specs=pl.BlockSpec((None, None, nb, d), lambda l, j: (l, j, 0, 0)),
        out_shape=jax.ShapeDtypeStruct((depth, six_d // d, nb, d), F32),
        compiler_params=_cparams("arbitrary", "arbitrary"),
        name="ada_mod",
    )(c_all, w_ada, b_ada.reshape(depth, six_d // d, 1, d))


def _rope(x, c, s1, s2):
    w = x.shape[-1]
    return x * c + pltpu.roll(x, 8, 1) * s1 + pltpu.roll(x, w - 8, 1) * s2


def _inproj_kernel(x_ref, sc_ref, sh_ref, wh_ref, wl_ref, rc_ref, rs1_ref, rs2_ref,
                   tc_ref, ts1_ref, ts2_ref,
                   zx_ref, zg_ref, q_ref, k_ref, v_ref, qi_ref, tail_ref, ki_ref):
    bb, tt, d = x_ref.shape
    h = x_ref[...] * (1.0 + sc_ref[...]) + sh_ref[...]
    hh, hl = _split(h.reshape(bb * tt, d))

    def proj(lo, hi):
        return _dot3s(hh, hl, wh_ref[:, lo:hi], wl_ref[:, lo:hi])

    lw = zx_ref.shape[-1]
    aw = q_ref.shape[-1]
    iw = qi_ref.shape[-1]
    o = 0
    zx_ref[...] = proj(o, o + lw); o += lw
    zg_ref[...] = proj(o, o + lw); o += lw
    rc, rs1, rs2 = rc_ref[...], rs1_ref[...], rs2_ref[...]
    rep = aw // LANE
    c4 = jnp.concatenate([rc] * rep, 1)
    s14 = jnp.concatenate([rs1] * rep, 1)
    s24 = jnp.concatenate([rs2] * rep, 1)
    q_ref[...] = _rope(proj(o, o + aw), c4, s14, s24); o += aw
    k_ref[...] = _rope(proj(o, o + aw), c4, s14, s24); o += aw
    v_ref[...] = proj(o, o + aw); o += aw
    repi = iw // LANE
    qi_ref[...] = _rope(proj(o, o + iw), jnp.concatenate([rc] * repi, 1),
                        jnp.concatenate([rs1] * repi, 1), jnp.concatenate([rs2] * repi, 1)); o += iw
    tail = _rope(proj(o, o + LANE), tc_ref[...], ts1_ref[...], ts2_ref[...])
    tail_ref[...] = tail
    ki_ref[...] = tail[:, :IDX_DIM]


def _rope_tables(pos):
    t = pos.shape[0]
    rot = HEAD_DIM // ROPE_FRAC_DIV
    half = rot // 2
    freqs = ROPE_THETA ** (-jnp.arange(half, dtype=F32) / half)
    ang = pos.astype(F32)[:, None] * freqs[None, :]
    cos, sin = jnp.cos(ang), jnp.sin(ang)
    ones = jnp.ones((t, HEAD_DIM - rot), F32)
    zeros = jnp.zeros((t, HEAD_DIM - rot), F32)
    zh = jnp.zeros((t, half), F32)
    c64 = jnp.concatenate([cos, cos, ones], 1)
    s1_64 = jnp.concatenate([zh, sin, zeros], 1)
    s2_64 = jnp.concatenate([-sin, zh, zeros], 1)
    rc = jnp.concatenate([c64, c64], 1)
    rs1 = jnp.concatenate([s1_64, s1_64], 1)
    rs2 = jnp.concatenate([s2_64, s2_64], 1)
    wscale = jnp.concatenate([jnp.full((t, IDX_HEADS), IDX_HEADS ** -0.5, F32),
                              jnp.ones((t, LANE - IDX_DIM - IDX_HEADS), F32)], 1)
    z64 = jnp.zeros((t, LANE - IDX_DIM), F32)
    tc = jnp.concatenate([c64, wscale], 1)
    ts1 = jnp.concatenate([s1_64, z64], 1)
    ts2 = jnp.concatenate([s2_64, z64], 1)
    return rc, rs1, rs2, tc, ts1, ts2


def _in_proj(x, mod, layer, w_hi, w_lo, tables, bb, tt):
    b, t, d = x.shape
    n = b * t
    rows = bb * tt
    nt = t // tt
    lw = d // 2
    aw = N_HEADS * HEAD_DIM
    iw = IDX_HEADS * IDX_DIM
    cols = w_hi.shape[-1]
    tab_idx = (lambda bi, ti: (ti, 0, 0)) if tables[0].shape[0] > 1 else (lambda bi, ti: (0, 0, 0))
    tab_spec = pl.BlockSpec((None, rows, LANE), tab_idx)
    row_idx = lambda bi, ti: (bi * nt + ti, 0)

    def mod_spec(j):
        return pl.BlockSpec((None, None, bb, 1, d), lambda bi, ti: (layer, j, bi, 0, 0))

    outs = [(n, lw), (n, lw), (n, aw), (n, aw), (n, aw), (n, iw), (n, LANE), (n, IDX_DIM)]
    return pl.pallas_call(
        _inproj_kernel,
        grid=(b // bb, nt),
        in_specs=[
            pl.BlockSpec((bb, tt, d), lambda bi, ti: (bi, ti, 0)),
            mod_spec(1), mod_spec(0),
            pl.BlockSpec((None, d, cols), lambda bi, ti: (layer, 0, 0)),
            pl.BlockSpec((None, d, cols), lambda bi, ti: (layer, 0, 0)),
        ] + [tab_spec] * 6,
        out_specs=[pl.BlockSpec((rows, w), row_idx) for _, w in outs],
        out_shape=[jax.ShapeDtypeStruct(s, F32) for s in outs],
        compiler_params=_cparams("arbitrary", "arbitrary"),
        name="in_proj",
    )(x, mod, mod, w_hi, w_lo, *tables)


def _lru_gates(xc, wah, wal, ba, wxh, wxl, bx, lam):
    xh, xl = _split(xc)
    r = jax.nn.sigmoid(_dot3s(xh, xl, wah, wal) + ba)
    i = jax.nn.sigmoid(_dot3s(xh, xl, wxh, wxl) + bx)
    nl = -lam
    sp = jnp.maximum(nl, 0.0) + jnp.log1p(jnp.exp(-jnp.abs(nl)))
    log_a = -LRU_C * r * sp
    a = jnp.exp(log_a)
    u = jnp.sqrt(1.0 - a * a) * (i * xc)
    return a, u


def _lru_prompt_kernel(zx_ref, zg_ref, cbuf_ref, h0_ref, cw_ref, cb_ref, wah_ref, wal_ref, ba_ref,
                       wxh_ref, wxl_ref, bx_ref, lam_ref, beta_ref,
                       mix_ref, nconv_ref, nh_ref, ext_ref, hst_ref):
    tt = zx_ref.shape[0]
    ti = pl.program_id(1)

    @pl.when(ti == 0)
    def _():
        ext_ref[5:8, :] = cbuf_ref[...]
        hst_ref[0:1, :] = h0_ref[...]

    x = zx_ref[...]
    ext_ref[8:8 + tt, :] = x
    w = cw_ref[...]
    xc = (cb_ref[...] + ext_ref[5:5 + tt, :] * w[0:1] + ext_ref[6:6 + tt, :] * w[1:2]
          + ext_ref[7:7 + tt, :] * w[2:3] + x * w[3:4])
    tail3 = ext_ref[tt + 5:tt + 8, :]
    ext_ref[5:8, :] = tail3

    a, u = _lru_gates(xc, wah_ref[...], wal_ref[...], ba_ref[...], wxh_ref[...], wxl_ref[...],
                      bx_ref[...], lam_ref[...])
    row = lax.broadcasted_iota(I32, (tt, 1), 0)
    s = 1
    while s < tt:
        a_sh = pltpu.roll(a, s, 0)
        u_sh = pltpu.roll(u, s, 0)
        m = row >= s
        u = u + a * jnp.where(m, u_sh, 0.0)
        a = a * jnp.where(m, a_sh, 1.0)
        s *= 2
    h = a * hst_ref[0:1, :] + u
    hst_ref[0:1, :] = h[tt - 1:tt, :]
    y = _gelu(zg_ref[...]) * h
    mix_ref[...] = _rms(y, beta_ref[...])

    @pl.when(ti == pl.num_programs(1) - 1)
    def _():
        nconv_ref[...] = tail3
        nh_ref[...] = h[tt - 1:tt, :]


def _lru_prompt(zx, zg, cbuf, h0, lw, b, t, tt):
    n, w = zx.shape
    nt = t // tt
    row_spec = pl.BlockSpec((tt, w), lambda bi, ti: (bi * nt + ti, 0))
    full = lambda shape: pl.BlockSpec(shape, lambda bi, ti: tuple(0 for _ in shape))
    vec = full((1, w))
    mat = full((w, w))
    return pl.pallas_call(
        _lru_prompt_kernel,
        grid=(b, nt),
        in_specs=[row_spec, row_spec,
                  pl.BlockSpec((None, CONV_W - 1, w), lambda bi, ti: (bi, 0, 0)),
                  pl.BlockSpec((None, 1, w), lambda bi, ti: (bi, 0, 0)),
                  full((CONV_W, w)), vec, mat, mat, vec, mat, mat, vec, vec, vec],
        out_specs=[row_spec,
                   pl.BlockSpec((None, CONV_W - 1, w), lambda bi, ti: (bi, 0, 0)),
                   pl.BlockSpec((None, 1, w), lambda bi, ti: (bi, 0, 0))],
        out_shape=[jax.ShapeDtypeStruct((n, w), F32),
                   jax.ShapeDtypeStruct((b, CONV_W - 1, w), F32),
                   jax.ShapeDtypeStruct((b, 1, w), F32)],
        scratch_shapes=[pltpu.VMEM((tt + 8, w), F32), pltpu.VMEM((8, w), F32)],
        compiler_params=_cparams("arbitrary", "arbitrary"),
        name="lru_prompt",
    )(zx, zg, cbuf, h0.reshape(b, 1, w), *lw)


def _lru_sample_kernel(zx_ref, zg_ref, cbuf_ref, h0_ref, cw_ref, cb_ref, wah_ref, wal_ref, ba_ref,
                       wxh_ref, wxl_ref, bx_ref, lam_ref, beta_ref,
                       mix_ref, nconv_ref, nh_ref, xc_ref):
    t, b, wd = zx_ref.shape
    w = cw_ref[...]
    xp = [cbuf_ref[k] for k in range(CONV_W - 1)] + [zx_ref[k] for k in range(t)]
    for k in range(t):
        xc_ref[k * b:(k + 1) * b, :] = (cb_ref[...] + xp[k] * w[0:1] + xp[k + 1] * w[1:2]
                                        + xp[k + 2] * w[2:3] + xp[k + 3] * w[3:4])
    for k in range(CONV_W - 1):
        nconv_ref[k] = xp[t + k]
    a, u = _lru_gates(xc_ref[...], wah_ref[...], wal_ref[...], ba_ref[...], wxh_ref[...],
                      wxl_ref[...], bx_ref[...], lam_ref[...])
    h = h0_ref[...]
    beta = beta_ref[...]
    for k in range(t):
        h = a[k * b:(k + 1) * b, :] * h + u[k * b:(k + 1) * b, :]
        mix_ref[k] = _rms(_gelu(zg_ref[k]) * h, beta)
    nh_ref[...] = h


def _lru_sample(zx_t, zg_t, cbuf_t, h0, lw):
    t, b, w = zx_t.shape
    full = lambda shape: pl.BlockSpec(shape, lambda i: tuple(0 for _ in shape))
    vec = full((1, w))
    mat = full((w, w))
    return pl.pallas_call(
        _lru_sample_kernel,
        grid=(1,),
        in_specs=[full((t, b, w)), full((t, b, w)), full((CONV_W - 1, b, w)), full((b, w)),
                  full((CONV_W, w)), vec, mat, mat, vec, mat, mat, vec, vec, vec],
        out_specs=[full((t, b, w)), full((CONV_W - 1, b, w)), full((b, w))],
        out_shape=[jax.ShapeDtypeStruct((t, b, w), F32),
                   jax.ShapeDtypeStruct((CONV_W - 1, b, w), F32),
                   jax.ShapeDtypeStruct((b, w), F32)],
        scratch_shapes=[pltpu.VMEM((t * b, w), F32)],
        compiler_params=_cparams("arbitrary"),
        name="lru_sample",
    )(zx_t, zg_t, cbuf_t, h0, *lw)


def _select_bias(score, qpos, key_ref, topk):
    q, s = score.shape
    kpos = lax.broadcasted_iota(I32, (q, s), 1)
    vis = kpos <= qpos
    key_ref[...] = jnp.where(vis, _order_key(score + 0.0), KEY_NEG_INF)

    def count_ge(t):
        return jnp.sum((key_ref[...] >= t).astype(F32), axis=1, keepdims=True)

    t0 = jnp.where(count_ge(jnp.zeros((q, 1), I32)) >= topk, 0, INT_MIN).astype(I32)

    def vbody(i, t):
        cand = t | jnp.left_shift(jnp.int32(1), 30 - i)
        return jnp.where(count_ge(cand) >= topk, cand, t)

    thr = lax.fori_loop(0, 31, vbody, t0)
    key = key_ref[...]
    c_gt = jnp.sum((key > thr).astype(F32), axis=1, keepdims=True)
    need = topk - c_gt
    nbits = max(1, (s - 1).bit_length())

    def ibody(i, j):
        cand = j + jnp.left_shift(jnp.int32(1), nbits - 1 - i)
        k2 = key_ref[...]
        kp = lax.broadcasted_iota(I32, (q, s), 1)
        f = jnp.sum(jnp.where(k2 == thr, (kp <= cand).astype(F32), 0.0), axis=1, keepdims=True)
        return jnp.where(f < need, cand, j)

    jsel = lax.fori_loop(0, nbits, ibody, jnp.full((q, 1), -1, I32)) + 1
    sel = jnp.where(key > thr, 1, jnp.where(key == thr, (kpos <= jsel).astype(I32), 0))
    sel = jnp.where(vis, sel, 0)
    return jnp.where(sel > 0, 0.0, NEG_INF)


def _attn_prompt_kernel(qi_ref, tailq_ref, tailk_ref, q_ref, k_ref, v_ref, o_ref, bias_ref, key_ref,
                        *, topk):
    tq = q_ref.shape[0]
    hd = pl.program_id(2)

    @pl.when(hd == 0)
    def _():
        tk = tailk_ref[...]
        kih, kil = _split(tk[:, :IDX_DIM])
        tqv = tailq_ref[...]
        qi = qi_ref[...]
        score = None
        for h in range(IDX_HEADS):
            qh, ql = _split(qi[:, h * IDX_DIM:(h + 1) * IDX_DIM])
            lg = _dot3s(qh, ql, kih, kil, NT)
            term = tqv[:, IDX_DIM + h:IDX_DIM + h + 1] * jnp.maximum(lg, 0.0)
            score = term if score is None else score + term
        qpos = pl.program_id(1) * tq + lax.broadcasted_iota(I32, (tq, 1), 0)
        bias_ref[...] = _select_bias(score, qpos, key_ref, topk)

    s = _dot(q_ref[...].astype(BF16), k_ref[...].astype(BF16), NT) * (HEAD_DIM ** -0.5) + bias_ref[...]
    m = jnp.max(s, axis=1, keepdims=True)
    p = jnp.exp(s - m)
    l = jnp.sum(p, axis=1, keepdims=True)
    o_ref[...] = _dot(p.astype(BF16), v_ref[...].astype(BF16)) / l


def _attn_prompt(qi, tail, q_hm, k_hm, v_hm, b, t, tq):
    nq = t // tq
    topk = min(TOPK_MAX, t // 4)
    iw = qi.shape[-1]
    hspec_q = pl.BlockSpec((None, None, tq, HEAD_DIM), lambda bi, qb, h: (bi, h, qb, 0))
    hspec_k = pl.BlockSpec((None, None, t, HEAD_DIM), lambda bi, qb, h: (bi, h, 0, 0))
    return pl.pallas_call(
        functools.partial(_attn_prompt_kernel, topk=topk),
        grid=(b, nq, N_HEADS),
        in_specs=[
            pl.BlockSpec((tq, iw), lambda bi, qb, h: (bi * nq + qb, 0)),
            pl.BlockSpec((tq, LANE), lambda bi, qb, h: (bi * nq + qb, 0)),
            pl.BlockSpec((t, LANE), lambda bi, qb, h: (bi, 0)),
            hspec_q, hspec_k, hspec_k,
        ],
        out_specs=hspec_q,
        out_shape=jax.ShapeDtypeStruct((b, N_HEADS, t, HEAD_DIM), F32),
        scratch_shapes=[pltpu.VMEM((tq, t), F32), pltpu.VMEM((tq, t), I32)],
        compiler_params=_cparams("arbitrary", "arbitrary", "arbitrary"),
        name="attn_prompt",
    )(qi, tail, tail, q_hm, k_hm, v_hm)


def _attn_sample_kernel(pt_ref, qi_ref, tail_ref, q_ref, kn_ref, vn_ref, *rest, n_pages, page, topk):
    ki_pages = rest[:n_pages]
    k_pages = rest[n_pages:2 * n_pages]
    v_pages = rest[2 * n_pages:3 * n_pages]
    o_ref, kis_ref, ks_ref, vs_ref, key_ref = rest[3 * n_pages:]
    t = q_ref.shape[0]
    past = n_pages * page
    aw = q_ref.shape[1]
    for j in range(n_pages):
        kis_ref[j * page:(j + 1) * page, :] = ki_pages[j][...]
        ks_ref[j * page:(j + 1) * page, :] = k_pages[j][...].astype(BF16)
        vs_ref[j * page:(j + 1) * page, :] = v_pages[j][...].astype(BF16)
    tl = tail_ref[...]
    kis_ref[past:past + page, :] = jnp.concatenate(
        [tl[:, :IDX_DIM], jnp.zeros((page - t, IDX_DIM), F32)], 0)
    ks_ref[past:past + page, :] = jnp.concatenate(
        [kn_ref[...], jnp.zeros((page - t, aw), F32)], 0).astype(BF16)
    vs_ref[past:past + page, :] = jnp.concatenate(
        [vn_ref[...], jnp.zeros((page - t, aw), F32)], 0).astype(BF16)

    qi = qi_ref[...]
    qi_rows = jnp.concatenate([qi[:, h * IDX_DIM:(h + 1) * IDX_DIM] for h in range(IDX_HEADS)], 0)
    w_rows = jnp.concatenate([tl[:, IDX_DIM + h:IDX_DIM + h + 1] for h in range(IDX_HEADS)], 0)
    qh, ql = _split(qi_rows)
    kh, kl = _split(kis_ref[...])
    sc = jnp.maximum(_dot3s(qh, ql, kh, kl, NT), 0.0) * w_rows
    score = sc[0:t]
    for h in range(1, IDX_HEADS):
        score = score + sc[h * t:(h + 1) * t]
    qpos = past + lax.broadcasted_iota(I32, (t, 1), 0)
    bias = _select_bias(score, qpos, key_ref, topk)

    q = q_ref[...]
    lane = lax.broadcasted_iota(I32, (t, aw), 1)
    head_masks = [(lane >= h * HEAD_DIM) & (lane < (h + 1) * HEAD_DIM) for h in range(N_HEADS)]
    qbd = jnp.concatenate([jnp.where(head_masks[h], q, 0.0) for h in range(N_HEADS)], 0)
    s = _dot(qbd.astype(BF16), ks_ref[...], NT) * (HEAD_DIM ** -0.5)
    s = s + jnp.concatenate([bias] * N_HEADS, 0)
    m = jnp.max(s, axis=1, keepdims=True)
    p = jnp.exp(s - m)
    l = jnp.sum(p, axis=1, keepdims=True)
    oall = _dot(p.astype(BF16), vs_ref[...]) / l
    o = jnp.where(head_masks[0], oall[0:t], 0.0)
    for h in range(1, N_HEADS):
        o = o + jnp.where(head_masks[h], oall[h * t:(h + 1) * t], 0.0)
    o_ref[...] = o


def _attn_sample(page_table, qi, tail, q, kn, vn, pool_ki, pool_k, pool_v, layer, b, t):
    n_pages = page_table.shape[1]
    page = pool_ki.shape[2]
    past = n_pages * page
    topk = min(TOPK_MAX, (past + t) // 4)
    aw = q.shape[-1]
    iw = qi.shape[-1]
    s_pad = past + page
    row = lambda w: pl.BlockSpec((t, w), lambda bi, pt: (bi, 0))

    def page_spec(w, j):
        return pl.BlockSpec((None, None, page, w), lambda bi, pt: (layer, pt[bi, j], 0, 0))

    in_specs = [row(iw), row(LANE), row(aw), row(aw), row(aw)]
    in_specs += [page_spec(IDX_DIM, j) for j in range(n_pages)]
    in_specs += [page_spec(aw, j) for j in range(n_pages)]
    in_specs += [page_spec(aw, j) for j in range(n_pages)]
    grid_spec = pltpu.PrefetchScalarGridSpec(
        num_scalar_prefetch=1,
        grid=(b,),
        in_specs=in_specs,
        out_specs=row(aw),
        scratch_shapes=[pltpu.VMEM((s_pad, IDX_DIM), F32), pltpu.VMEM((s_pad, aw), BF16),
                        pltpu.VMEM((s_pad, aw), BF16), pltpu.VMEM((t, s_pad), I32)],
    )
    return pl.pallas_call(
        functools.partial(_attn_sample_kernel, n_pages=n_pages, page=page, topk=topk),
        grid_spec=grid_spec,
        out_shape=jax.ShapeDtypeStruct((b * t, aw), F32),
        compiler_params=_cparams("arbitrary"),
        name="attn_sample",
    )(page_table, qi, tail, q, kn, vn, *([pool_ki] * n_pages), *([pool_k] * n_pages),
      *([pool_v] * n_pages))


def _outproj_kernel(x_ref, g_ref, ml_ref, o_ref, beta_ref, wh_ref, wl_ref, lg_ref, lb_ref, y_ref):
    bb, tt, d = x_ref.shape
    lw = ml_ref.shape[-1]
    mo = _rms(o_ref[...], beta_ref[...])
    acc = (_dot3w(ml_ref[...], wh_ref[0:lw, :], wl_ref[0:lw, :])
           + _dot3w(mo, wh_ref[lw:, :], wl_ref[lw:, :]))
    y = ALPHA * x_ref[...] + g_ref[...] * acc.reshape(bb, tt, d)
    y_ref[...] = _layer_norm(y, lg_ref[...], lb_ref[...])


def _out_proj(x, mod, layer, mix_lru, o, beta_attn, w_hi, w_lo, ln_g, ln_b, bb, tt):
    b, t, d = x.shape
    nt = t // tt
    rows = bb * tt
    lw = mix_lru.shape[-1]
    aw = o.shape[-1]
    xspec = pl.BlockSpec((bb, tt, d), lambda bi, ti: (bi, ti, 0))
    vec = lambda w: pl.BlockSpec((1, w), lambda bi, ti: (0, 0))
    wspec = pl.BlockSpec((None, lw + aw, d), lambda bi, ti: (layer, 0, 0))
    return pl.pallas_call(
        _outproj_kernel,
        grid=(b // bb, nt),
        in_specs=[xspec,
                  pl.BlockSpec((None, None, bb, 1, d), lambda bi, ti: (layer, 2, bi, 0, 0)),
                  pl.BlockSpec((rows, lw), lambda bi, ti: (bi * nt + ti, 0)),
                  pl.BlockSpec((rows, aw), lambda bi, ti: (bi * nt + ti, 0)),
                  vec(aw), wspec, wspec, vec(d), vec(d)],
        out_specs=xspec,
        out_shape=jax.ShapeDtypeStruct((b, t, d), F32),
        compiler_params=_cparams("arbitrary", "arbitrary"),
        name="out_proj",
    )(x, mod, mix_lru, o, beta_attn, w_hi, w_lo, ln_g, ln_b)


def _extract_top(cur, n_take, on_take):
    rows = cur.shape[0]
    ridx = lax.broadcasted_iota(I32, cur.shape, 0)
    for r in range(n_take):
        m = jnp.max(cur, axis=0, keepdims=True)
        jmin = jnp.min(jnp.where(cur == m, ridx, rows), axis=0, keepdims=True)
        on_take(r, m)
        cur = jnp.where(ridx == jmin, NEG_INF, cur)
    return cur


def _peer_kernel(x_ref, sc_ref, sh_ref, g_ref, wqh_ref, wql_ref, kh_ref, kl_ref, u_ref, v_ref,
                 lg_ref, lb_ref, y_ref,
                 hb_ref, hl_ref, st_ref, e_ref, sv_ref, tau_ref, acc_ref):
    bb, tt, d = x_ref.shape
    tn = bb * tt
    nk = st_ref.shape[1]
    ei = pl.program_id(2)

    @pl.when(ei == 0)
    def _():
        h2 = (x_ref[...] * (1.0 + sc_ref[...]) + sh_ref[...]).reshape(tn, d)
        hh, hl = _split(h2)
        hb_ref[...] = hh
        hl_ref[...] = hl
        acc_ref[...] = jnp.zeros_like(acc_ref)

        def stage1(hp, carry):
            qs = _dot3s(hb_ref[...], hl_ref[...], wqh_ref[hp], wql_ref[hp])
            qh, ql = _split(qs)
            s_t = _dot3s(kh_ref[hp], kl_ref[hp], qh, ql, NT)
            st_ref[hp] = s_t

            def take(r, m):
                sv_ref[hp, r:r + 1, :] = m

            _extract_top(s_t, PEER_TOPK, take)
            return carry

        lax.fori_loop(0, 2 * PEER_HEADS, stage1, 0)

        def stage2(h, carry):
            sv0 = sv_ref[2 * h]
            sv1 = sv_ref[2 * h + 1]
            m0 = sv0[0:1]
            m1 = sv1[0:1]
            blocks = []
            for a in range(PEER_TOPK):
                nb = PEER_TOPK // (a + 1)
                nb = min(PEER_TOPK, -(-nb // 8) * 8)
                blocks.append(sv0[a:a + 1] + sv1[0:nb])
            cand = jnp.concatenate(blocks, 0)
            mtot = m0 + m1
            acc = {"z": jnp.zeros_like(m0), "tau": m0}

            def take(r, m):
                acc["z"] = acc["z"] + jnp.exp(m - mtot)
                acc["tau"] = m

            _extract_top(cand, PEER_TOPK, take)
            tau_ref[pl.ds(h, 1), :] = acc["tau"]
            inv_z = 1.0 / acc["z"]
            e_ref[2 * h] = jnp.exp(st_ref[2 * h] - m0)
            e_ref[2 * h + 1] = jnp.exp(st_ref[2 * h + 1] - m1) * inv_z
            return carry

        lax.fori_loop(0, PEER_HEADS, stage2, 0)

    act = _gelu(_dot(u_ref[...], hb_ref[...], NT))
    gate = jnp.zeros((nk, tn), F32)
    for h in range(PEER_HEADS):
        s0 = st_ref[2 * h, pl.ds(ei, 1), :]
        e0 = e_ref[2 * h, pl.ds(ei, 1), :]
        ssum = st_ref[2 * h + 1] + s0
        gate = gate + jnp.where(ssum >= tau_ref[h:h + 1, :], e_ref[2 * h + 1] * e0, 0.0)
    w_t = (gate * act).astype(BF16)
    acc_ref[...] += _dot(w_t, v_ref[...], TN)

    @pl.when(ei == pl.num_programs(2) - 1)
    def _():
        y = ALPHA * x_ref[...] + g_ref[...] * acc_ref[...].reshape(bb, tt, d)
        y_ref[...] = _layer_norm(y, lg_ref[...], lb_ref[...])


def _peer(x, mod, layer, wq_hi, wq_lo, k_hi, k_lo, u_bf, v_bf, ln_g, ln_b, bb, tt):
    b, t, d = x.shape
    nt = t // tt
    tn = bb * tt
    nhp = 2 * PEER_HEADS
    dk = wq_hi.shape[-1]
    nk = k_hi.shape[2]
    xspec = pl.BlockSpec((bb, tt, d), lambda bi, ti, ei: (bi, ti, 0))
    vec = pl.BlockSpec((1, d), lambda bi, ti, ei: (0, 0))

    def mod_spec(j):
        return pl.BlockSpec((None, None, bb, 1, d), lambda bi, ti, ei: (layer, j, bi, 0, 0))

    wq_spec = pl.BlockSpec((None, nhp, d, dk), lambda bi, ti, ei: (layer, 0, 0, 0))
    k_spec = pl.BlockSpec((None, nhp, nk, dk), lambda bi, ti, ei: (layer, 0, 0, 0))
    tab_spec = pl.BlockSpec((None, nk, d), lambda bi, ti, ei: (layer, ei, 0))
    return pl.pallas_call(
        _peer_kernel,
        grid=(b // bb, nt, nk),
        in_specs=[xspec, mod_spec(4), mod_spec(3), mod_spec(5), wq_spec, wq_spec, k_spec, k_spec,
                  tab_spec, tab_spec, vec, vec],
        out_specs=xspec,
        out_shape=jax.ShapeDtypeStruct((b, t, d), F32),
        scratch_shapes=[pltpu.VMEM((tn, d), BF16), pltpu.VMEM((tn, d), BF16),
                        pltpu.VMEM((nhp, nk, tn), F32), pltpu.VMEM((nhp, nk, tn), F32),
                        pltpu.VMEM((nhp, PEER_TOPK, tn), F32), pltpu.VMEM((PEER_HEADS, tn), F32),
                        pltpu.VMEM((tn, d), F32)],
        compiler_params=_cparams("arbitrary", "arbitrary", "arbitrary"),
        name="peer",
    )(x, mod, mod, mod, wq_hi, wq_lo, k_hi, k_lo, u_bf, v_bf, ln_g, ln_b)


def _block_diag(w):
    l, nb, bw, _ = w.shape
    eye = jnp.eye(nb, dtype=w.dtype)
    return jnp.einsum("lncd,nm->lncmd", w, eye).reshape(l, nb * bw, nb * bw)


def _pick_tile(t, target):
    tt = min(t, target)
    while t % tt:
        tt //= 2
    return tt


def kernel(x_prompt, x_sample, cache_k, cache_v, cache_idx_k, state_conv, state_lru, page_table,
           c_prompt, c_sample, w_ada, b_ada, w_in, conv_w, conv_b, lru_wa, lru_ba, lru_wx, lru_bx,
           lru_lambda, beta_lru, beta_attn, w_out, ln1_g, ln1_b, ln2_g, ln2_b, peer_wq, peer_keys,
           peer_u, peer_v):
    depth = w_ada.shape[0]
    bp, tp, d = x_prompt.shape
    bs, ts, _ = x_sample.shape
    lw = d // 2
    aw = N_HEADS * HEAD_DIM
    n_pool, page = cache_k.shape[1], cache_k.shape[2]
    past = page_table.shape[1] * page
    assert tp >= CONV_W - 1 and ts >= CONV_W - 1 and ts % 8 == 0 and bs % 8 == 0

    in_cols = w_in.shape[-1]
    cols_pad = -(-in_cols // LANE) * LANE
    w_in_hi, w_in_lo = _split(jnp.pad(w_in, ((0, 0), (0, 0), (0, cols_pad - in_cols))))
    w_out_hi, w_out_lo = _split(w_out)
    wa_hi, wa_lo = _split(_block_diag(lru_wa))
    wx_hi, wx_lo = _split(_block_diag(lru_wx))
    nhp = 2 * PEER_HEADS
    dk = peer_wq.shape[-1] // nhp
    wq = peer_wq.reshape(depth, d, nhp, dk).transpose(0, 2, 1, 3)
    wq_hi, wq_lo = _split(wq)
    nk = peer_keys.shape[3]
    pk = peer_keys.transpose(0, 2, 1, 3, 4).reshape(depth, nhp, nk, dk)
    pk_hi, pk_lo = _split(pk)
    u_bf = peer_u.astype(BF16)
    v_bf = peer_v.astype(BF16)
    pool_ki = cache_idx_k
    pool_k = cache_k.reshape(depth, n_pool, page, aw)
    pool_v = cache_v.reshape(depth, n_pool, page, aw)

    nb_all = bp + bs
    nb_pad = -(-nb_all // 8) * 8
    c_all = jnp.pad(jnp.concatenate([c_prompt, c_sample], 0), ((0, nb_pad - nb_all), (0, 0)))
    mod = _ada_mod(c_all, w_ada, b_ada)
    mod_p = mod[:, :, :bp].reshape(depth, 6, bp, 1, d)
    mod_s = mod[:, :, bp:nb_all].reshape(depth, 6, bs, 1, d)

    tt_p = _pick_tile(tp, 512)
    tabs_p = [a.reshape(tp // tt_p, tt_p, LANE) for a in _rope_tables(jnp.arange(tp, dtype=I32))]
    bb_s = _pick_tile(bs, 32)
    tabs_s = [jnp.tile(a, (bb_s, 1)).reshape(1, bb_s * ts, LANE)
              for a in _rope_tables(past + jnp.arange(ts, dtype=I32))]

    tt_lru = _pick_tile(tp, 256)
    tq = _pick_tile(tp, Q_BLOCK)
    tn_p = _pick_tile(tp, 512)
    bb_peer = _pick_tile(bs, max(1, 512 // ts))

    yp, ys = x_prompt, x_sample
    outs = {k: [] for k in ("kp", "vp", "kip", "cp", "hp", "ks", "vs", "kis", "cs", "hs")}
    for l in range(depth):
        lru_w = (conv_w[l], conv_b[l][None], wa_hi[l], wa_lo[l], lru_ba[l][None], wx_hi[l], wx_lo[l],
                 lru_bx[l][None], lru_lambda[l][None], beta_lru[l][None])
        zx, zg, q, k, v, qi, tail, ki = _in_proj(yp, mod_p, l, w_in_hi, w_in_lo, tabs_p, 1, tt_p)
        zc = jnp.zeros((bp, CONV_W - 1, lw), F32)
        zh = jnp.zeros((bp, lw), F32)
        mix_lru, nconv, nh = _lru_prompt(zx, zg, zc, zh, lru_w, bp, tp, tt_lru)
        to_hm = lambda a: a.reshape(bp, tp, N_HEADS, HEAD_DIM).transpose(0, 2, 1, 3)
        o_hm = _attn_prompt(qi, tail, to_hm(q), to_hm(k), to_hm(v), bp, tp, tq)
        o = o_hm.transpose(0, 2, 1, 3).reshape(bp * tp, aw)
        x1 = _out_proj(yp, mod_p, l, mix_lru, o, beta_attn[l][None], w_out_hi, w_out_lo,
                       ln1_g[l][None], ln1_b[l][None], 1, tt_p)
        yp = _peer(x1, mod_p, l, wq_hi, wq_lo, pk_hi, pk_lo, u_bf, v_bf,
                   ln2_g[l][None], ln2_b[l][None], 1, tn_p)
        outs["kp"].append(k.reshape(bp, tp, N_HEADS, HEAD_DIM))
        outs["vp"].append(v.reshape(bp, tp, N_HEADS, HEAD_DIM))
        outs["kip"].append(ki.reshape(bp, tp, IDX_DIM))
        outs["cp"].append(nconv)
        outs["hp"].append(nh.reshape(bp, lw))
        zx, zg, q, k, v, qi, tail, ki = _in_proj(ys, mod_s, l, w_in_hi, w_in_lo, tabs_s, bb_s, ts)
        tm = lambda a: a.reshape(bs, ts, lw).transpose(1, 0, 2)
        mix_t, nconv_t, nh = _lru_sample(tm(zx), tm(zg), state_conv[l].transpose(1, 0, 2),
                                         state_lru[l], lru_w)
        mix_lru = mix_t.transpose(1, 0, 2).reshape(bs * ts, lw)
        o = _attn_sample(page_table, qi, tail, q, k, v, pool_ki, pool_k, pool_v, l, bs, ts)
        x1 = _out_proj(ys, mod_s, l, mix_lru, o, beta_attn[l][None], w_out_hi, w_out_lo,
                       ln1_g[l][None], ln1_b[l][None], bb_s, ts)
        ys = _peer(x1, mod_s, l, wq_hi, wq_lo, pk_hi, pk_lo, u_bf, v_bf,
                   ln2_g[l][None], ln2_b[l][None], bb_peer, ts)
        outs["ks"].append(k.reshape(bs, ts, N_HEADS, HEAD_DIM))
        outs["vs"].append(v.reshape(bs, ts, N_HEADS, HEAD_DIM))
        outs["kis"].append(ki.reshape(bs, ts, IDX_DIM))
        outs["cs"].append(nconv_t.transpose(1, 0, 2))
        outs["hs"].append(nh)

    st = lambda name: jnp.stack(outs[name])
    return (yp, ys, st("kp"), st("vp"), st("kip"), st("cp"), st("hp"),
            st("ks"), st("vs"), st("kis"), st("cs"), st("hs"))
```

```python
import functools
import math

import jax
import jax.numpy as jnp
from jax import lax
from jax.experimental import pallas as pl
from jax.experimental.pallas import tpu as pltpu

F32 = jnp.float32
BF16 = jnp.bfloat16
I32 = jnp.int32

LRU_C = 8.0
CONV_W = 4
N_HEADS = 8
HEAD_DIM = 64
IDX_HEADS = 4
IDX_DIM = 64
TOPK_MAX = 256
ROPE_THETA = 500000.0
ROPE_FRAC_DIV = 4
Q_BLOCK = 128
PEER_HEADS = 8
PEER_TOPK = 16
DEPTH_NOMINAL = 4
ALPHA = (2 * DEPTH_NOMINAL) ** 0.25
LN_EPS = 1e-5

LANE = 128
VMEM_LIMIT = 56 * 1024 * 1024

NN = (((1,), (0,)), ((), ()))
NT = (((1,), (1,)), ((), ()))
TN = (((0,), (0,)), ((), ()))

NEG_INF = float("-inf")
INT_MIN = -2147483648
KEY_NEG_INF = -2139095041


def _cparams(*sem):
    return pltpu.CompilerParams(dimension_semantics=sem, vmem_limit_bytes=VMEM_LIMIT)


def _split(a):
    hi = a.astype(BF16)
    lo = (a - hi.astype(F32)).astype(BF16)
    return hi, lo


def _dot(a, b, dims=NN):
    return lax.dot_general(a, b, dims, preferred_element_type=F32)


def _dot3s(ah, al, bh, bl, dims=NN):
    return _dot(ah, bh, dims) + (_dot(ah, bl, dims) + _dot(al, bh, dims))


def _dot3w(a, bh, bl, dims=NN):
    ah, al = _split(a)
    return _dot3s(ah, al, bh, bl, dims)


def _cat3(x, lhs):
    hi = x.astype(BF16).astype(F32)
    lo = (x - hi).astype(BF16).astype(F32)
    parts = [hi, hi, lo] if lhs else [hi, lo, hi]
    return jnp.concatenate(parts, 1).astype(BF16)


def _gelu(x):
    return 0.5 * x * (1.0 + lax.erf(x * (1.0 / math.sqrt(2.0))))


def _layer_norm(y, g, b):
    mu = jnp.mean(y, -1, keepdims=True)
    d = y - mu
    var = jnp.mean(d * d, -1, keepdims=True)
    return d * lax.rsqrt(var + LN_EPS) * g + b


def _rms(y, g):
    return y * lax.rsqrt(jnp.mean(y * y, -1, keepdims=True) + LN_EPS) * g


def _order_key(x):
    b = pltpu.bitcast(x, I32)
    return jnp.where(b < 0, b ^ 0x7FFFFFFF, b)


def _ada_kernel(c_ref, w_ref, b_ref, o_ref):
    c = c_ref[...]
    s = c * jax.nn.sigmoid(c)
    wh, wl = _split(w_ref[...])
    o_ref[...] = _dot3w(s, wh, wl) + b_ref[...]


def _ada_mod(c_all, w_ada, b_ada):
    depth, d, six_d = w_ada.shape
    nb = c_all.shape[0]
    return pl.pallas_call(
        _ada_kernel,
        grid=(depth, six_d // d),
        in_specs=[
            pl.BlockSpec((nb, d), lambda l, j: (0, 0)),
            pl.BlockSpec((None, d, d), lambda l, j: (l, 0, j)),
            pl.BlockSpec((None, None, 1, d), lambda l, j: (l, j, 0, 0)),
        ],
        out_specs=pl.BlockSpec((None, None, nb, d), lambda l, j: (l, j, 0, 0)),
        out_shape=jax.ShapeDtypeStruct((depth, six_d // d, nb, d), F32),
        compiler_params=_cparams("arbitrary", "arbitrary"),
        name="ada_mod",
    )(c_all, w_ada, b_ada.reshape(depth, six_d // d, 1, d))


def _rope(x, c, s1, s2):
    w = x.shape[-1]
    return x * c + pltpu.roll(x, 8, 1) * s1 + pltpu.roll(x, w - 8, 1) * s2


def _inproj_kernel(x_ref, sc_ref, sh_ref, wh_ref, wl_ref, rc_ref, rs1_ref, rs2_ref,
                   tc_ref, ts1_ref, ts2_ref,
                   zx_ref, zg_ref, q_ref, k_ref, v_ref, qi_ref, tail_ref, ki_ref, ki3_ref):
    bb, tt, d = x_ref.shape
    h = x_ref[...] * (1.0 + sc_ref[...]) + sh_ref[...]
    hh, hl = _split(h.reshape(bb * tt, d))

    def proj(lo, hi):
        return _dot3s(hh, hl, wh_ref[:, lo:hi], wl_ref[:, lo:hi])

    lw = zx_ref.shape[-1]
    aw = q_ref.shape[-1]
    iw = qi_ref.shape[-1]
    o = 0
    zx_ref[...] = proj(o, o + lw); o += lw
    zg_ref[...] = proj(o, o + lw); o += lw
    rc, rs1, rs2 = rc_ref[...], rs1_ref[...], rs2_ref[...]
    rep = aw // LANE
    c4 = jnp.concatenate([rc] * rep, 1)
    s14 = jnp.concatenate([rs1] * rep, 1)
    s24 = jnp.concatenate([rs2] * rep, 1)
    q_ref[...] = _rope(proj(o, o + aw), c4, s14, s24); o += aw
    k_ref[...] = _rope(proj(o, o + aw), c4, s14, s24); o += aw
    v_ref[...] = proj(o, o + aw); o += aw
    repi = iw // LANE
    qi_ref[...] = _rope(proj(o, o + iw), jnp.concatenate([rc] * repi, 1),
                        jnp.concatenate([rs1] * repi, 1), jnp.concatenate([rs2] * repi, 1)); o += iw
    tail = _rope(proj(o, o + LANE), tc_ref[...], ts1_ref[...], ts2_ref[...])
    tail_ref[...] = tail
    ki_ref[...] = tail[:, :IDX_DIM]
    ki3_ref[...] = _cat3(tail[:, :IDX_DIM], lhs=False)


def _rope_tables(pos):
    t = pos.shape[0]
    rot = HEAD_DIM // ROPE_FRAC_DIV
    half = rot // 2
    freqs = ROPE_THETA ** (-jnp.arange(half, dtype=F32) / half)
    ang = pos.astype(F32)[:, None] * freqs[None, :]
    cos, sin = jnp.cos(ang), jnp.sin(ang)
    ones = jnp.ones((t, HEAD_DIM - rot), F32)
    zeros = jnp.zeros((t, HEAD_DIM - rot), F32)
    zh = jnp.zeros((t, half), F32)
    c64 = jnp.concatenate([cos, cos, ones], 1)
    s1_64 = jnp.concatenate([zh, sin, zeros], 1)
    s2_64 = jnp.concatenate([-sin, zh, zeros], 1)
    rc = jnp.concatenate([c64, c64], 1)
    rs1 = jnp.concatenate([s1_64, s1_64], 1)
    rs2 = jnp.concatenate([s2_64, s2_64], 1)
    wscale = jnp.concatenate([jnp.full((t, IDX_HEADS), IDX_HEADS ** -0.5, F32),
                              jnp.ones((t, LANE - IDX_DIM - IDX_HEADS), F32)], 1)
    z64 = jnp.zeros((t, LANE - IDX_DIM), F32)
    tc = jnp.concatenate([c64, wscale], 1)
    ts1 = jnp.concatenate([s1_64, z64], 1)
    ts2 = jnp.concatenate([s2_64, z64], 1)
    return rc, rs1, rs2, tc, ts1, ts2


def _in_proj(x, mod, layer, w_hi, w_lo, tables, bb, tt):
    b, t, d = x.shape
    n = b * t
    rows = bb * tt
    nt = t // tt
    lw = d // 2
    aw = N_HEADS * HEAD_DIM
    iw = IDX_HEADS * IDX_DIM
    cols = w_hi.shape[-1]
    tab_idx = (lambda bi, ti: (ti, 0, 0)) if tables[0].shape[0] > 1 else (lambda bi, ti: (0, 0, 0))
    tab_spec = pl.BlockSpec((None, rows, LANE), tab_idx)
    row_idx = lambda bi, ti: (bi * nt + ti, 0)

    def mod_spec(j):
        return pl.BlockSpec((None, None, bb, 1, d), lambda bi, ti: (layer, j, bi, 0, 0))

    outs = [(n, lw), (n, lw), (n, aw), (n, aw), (n, aw), (n, iw), (n, LANE), (n, IDX_DIM),
            (n, 3 * IDX_DIM)]
    dtypes = [F32] * 8 + [BF16]
    return pl.pallas_call(
        _inproj_kernel,
        grid=(b // bb, nt),
        in_specs=[
            pl.BlockSpec((bb, tt, d), lambda bi, ti: (bi, ti, 0)),
            mod_spec(1), mod_spec(0),
            pl.BlockSpec((None, d, cols), lambda bi, ti: (layer, 0, 0)),
            pl.BlockSpec((None, d, cols), lambda bi, ti: (layer, 0, 0)),
        ] + [tab_spec] * 6,
        out_specs=[pl.BlockSpec((rows, w), row_idx) for _, w in outs],
        out_shape=[jax.ShapeDtypeStruct(s, dt) for s, dt in zip(outs, dtypes)],
        compiler_params=_cparams("arbitrary", "arbitrary"),
        name="in_proj",
    )(x, mod, mod, w_hi, w_lo, *tables)


def _lru_gates(xc, wah, wal, ba, wxh, wxl, bx, lam):
    xh, xl = _split(xc)
    r = jax.nn.sigmoid(_dot3s(xh, xl, wah, wal) + ba)
    i = jax.nn.sigmoid(_dot3s(xh, xl, wxh, wxl) + bx)
    nl = -lam
    sp = jnp.maximum(nl, 0.0) + jnp.log1p(jnp.exp(-jnp.abs(nl)))
    log_a = -LRU_C * r * sp
    a = jnp.exp(log_a)
    u = jnp.sqrt(1.0 - a * a) * (i * xc)
    return a, u


def _lru_prompt_kernel(zx_ref, zg_ref, cbuf_ref, h0_ref, cw_ref, cb_ref, wah_ref, wal_ref, ba_ref,
                       wxh_ref, wxl_ref, bx_ref, lam_ref, beta_ref,
                       mix_ref, nconv_ref, nh_ref, ext_ref, hst_ref):
    tt = zx_ref.shape[0]
    ti = pl.program_id(1)

    @pl.when(ti == 0)
    def _():
        ext_ref[5:8, :] = cbuf_ref[...]
        hst_ref[0:1, :] = h0_ref[...]

    x = zx_ref[...]
    ext_ref[8:8 + tt, :] = x
    w = cw_ref[...]
    xc = (cb_ref[...] + ext_ref[5:5 + tt, :] * w[0:1] + ext_ref[6:6 + tt, :] * w[1:2]
          + ext_ref[7:7 + tt, :] * w[2:3] + x * w[3:4])
    tail3 = ext_ref[tt + 5:tt + 8, :]
    ext_ref[5:8, :] = tail3

    a, u = _lru_gates(xc, wah_ref[...], wal_ref[...], ba_ref[...], wxh_ref[...], wxl_ref[...],
                      bx_ref[...], lam_ref[...])
    row = lax.broadcasted_iota(I32, (tt, 1), 0)
    s = 1
    while s < tt:
        a_sh = pltpu.roll(a, s, 0)
        u_sh = pltpu.roll(u, s, 0)
        m = row >= s
        u = u + a * jnp.where(m, u_sh, 0.0)
        a = a * jnp.where(m, a_sh, 1.0)
        s *= 2
    h = a * hst_ref[0:1, :] + u
    hst_ref[0:1, :] = h[tt - 1:tt, :]
    y = _gelu(zg_ref[...]) * h
    mix_ref[...] = _rms(y, beta_ref[...])

    @pl.when(ti == pl.num_programs(1) - 1)
    def _():
        nconv_ref[...] = tail3
        nh_ref[...] = h[tt - 1:tt, :]


def _lru_prompt(zx, zg, cbuf, h0, lw, b, t, tt):
    n, w = zx.shape
    nt = t // tt
    row_spec = pl.BlockSpec((tt, w), lambda bi, ti: (bi * nt + ti, 0))
    full = lambda shape: pl.BlockSpec(shape, lambda bi, ti: tuple(0 for _ in shape))
    vec = full((1, w))
    mat = full((w, w))
    return pl.pallas_call(
        _lru_prompt_kernel,
        grid=(b, nt),
        in_specs=[row_spec, row_spec,
                  pl.BlockSpec((None, CONV_W - 1, w), lambda bi, ti: (bi, 0, 0)),
                  pl.BlockSpec((None, 1, w), lambda bi, ti: (bi, 0, 0)),
                  full((CONV_W, w)), vec, mat, mat, vec, mat, mat, vec, vec, vec],
        out_specs=[row_spec,
                   pl.BlockSpec((None, CONV_W - 1, w), lambda bi, ti: (bi, 0, 0)),
                   pl.BlockSpec((None, 1, w), lambda bi, ti: (bi, 0, 0))],
        out_shape=[jax.ShapeDtypeStruct((n, w), F32),
                   jax.ShapeDtypeStruct((b, CONV_W - 1, w), F32),
                   jax.ShapeDtypeStruct((b, 1, w), F32)],
        scratch_shapes=[pltpu.VMEM((tt + 8, w), F32), pltpu.VMEM((8, w), F32)],
        compiler_params=_cparams("arbitrary", "arbitrary"),
        name="lru_prompt",
    )(zx, zg, cbuf, h0.reshape(b, 1, w), *lw)


def _lru_sample_kernel(zx_ref, zg_ref, cbuf_ref, h0_ref, cw_ref, cb_ref, wah_ref, wal_ref, ba_ref,
                       wxh_ref, wxl_ref, bx_ref, lam_ref, beta_ref,
                       mix_ref, nconv_ref, nh_ref, xc_ref):
    t, b, wd = zx_ref.shape
    w = cw_ref[...]
    xp = [cbuf_ref[k] for k in range(CONV_W - 1)] + [zx_ref[k] for k in range(t)]
    for k in range(t):
        xc_ref[k * b:(k + 1) * b, :] = (cb_ref[...] + xp[k] * w[0:1] + xp[k + 1] * w[1:2]
                                        + xp[k + 2] * w[2:3] + xp[k + 3] * w[3:4])
    for k in range(CONV_W - 1):
        nconv_ref[k] = xp[t + k]
    a, u = _lru_gates(xc_ref[...], wah_ref[...], wal_ref[...], ba_ref[...], wxh_ref[...],
                      wxl_ref[...], bx_ref[...], lam_ref[...])
    h = h0_ref[...]
    beta = beta_ref[...]
    for k in range(t):
        h = a[k * b:(k + 1) * b, :] * h + u[k * b:(k + 1) * b, :]
        mix_ref[k] = _rms(_gelu(zg_ref[k]) * h, beta)
    nh_ref[...] = h


def _lru_sample(zx_t, zg_t, cbuf_t, h0, lw):
    t, b, w = zx_t.shape
    full = lambda shape: pl.BlockSpec(shape, lambda i: tuple(0 for _ in shape))
    vec = full((1, w))
    mat = full((w, w))
    return pl.pallas_call(
        _lru_sample_kernel,
        grid=(1,),
        in_specs=[full((t, b, w)), full((t, b, w)), full((CONV_W - 1, b, w)), full((b, w)),
                  full((CONV_W, w)), vec, mat, mat, vec, mat, mat, vec, vec, vec],
        out_specs=[full((t, b, w)), full((CONV_W - 1, b, w)), full((b, w))],
        out_shape=[jax.ShapeDtypeStruct((t, b, w), F32),
                   jax.ShapeDtypeStruct((CONV_W - 1, b, w), F32),
                   jax.ShapeDtypeStruct((b, w), F32)],
        scratch_shapes=[pltpu.VMEM((t * b, w), F32)],
        compiler_params=_cparams("arbitrary"),
        name="lru_sample",
    )(zx_t, zg_t, cbuf_t, h0, *lw)


def _select_bias(score, qpos, key_ref, topk):
    q, s = score.shape
    kpos = lax.broadcasted_iota(I32, (q, s), 1)
    vis = kpos <= qpos
    key_ref[...] = jnp.where(vis, _order_key(score + 0.0), KEY_NEG_INF)

    kf = float(topk)

    def count_ge(t):
        return jnp.sum(jnp.where(key_ref[...] >= t, 1.0, 0.0), axis=1, keepdims=True)

    c0 = count_ge(jnp.zeros((q, 1), I32))
    small = (qpos + 1) <= topk
    t0 = jnp.where(small, KEY_NEG_INF + 1, jnp.where(c0 >= kf, 0, INT_MIN)).astype(I32)
    r0 = jnp.where(small, 1, jnp.where(c0 == kf, 1, 0)).astype(I32)

    def vcond(c):
        i, _, res = c
        return jnp.logical_and(i < 31, jnp.min(res) == 0)

    def vbody(c):
        i, t, res = c
        cand = t | jnp.left_shift(jnp.int32(1), 30 - i)
        cnt = count_ge(cand)
        t = jnp.where(res > 0, t, jnp.where(cnt >= kf, cand, t))
        res = jnp.where(cnt == kf, 1, res)
        return i + 1, t, res

    _, thr, res = lax.while_loop(vcond, vbody, (jnp.int32(0), t0, r0))
    nbits = max(1, (s - 1).bit_length())

    def tie_phase():
        c_gt = jnp.sum(jnp.where(key_ref[...] > thr, 1.0, 0.0), axis=1, keepdims=True)
        need = kf - c_gt

        def ibody(i, j):
            cand = j + jnp.left_shift(jnp.int32(1), nbits - 1 - i)
            kp = lax.broadcasted_iota(I32, (q, s), 1)
            hit = jnp.where(key_ref[...] == thr, jnp.where(kp <= cand, 1.0, 0.0), 0.0)
            f = jnp.sum(hit, axis=1, keepdims=True)
            return jnp.where(f < need, cand, j)

        return lax.fori_loop(0, nbits, ibody, jnp.full((q, 1), -1, I32)) + 1

    jsel = lax.cond(jnp.min(res) == 0, tie_phase, lambda: jnp.full((q, 1), s, I32))
    jsel = jnp.where(res > 0, s, jsel)
    key = key_ref[...]
    sel = jnp.where(key > thr, 1, jnp.where(key == thr, jnp.where(kpos <= jsel, 1, 0), 0))
    sel = jnp.where(vis, sel, 0)
    return jnp.where(sel > 0, 0.0, NEG_INF)


def _attn_prompt_kernel(qi_ref, tailq_ref, ki3_ref, q_ref, k_ref, v_ref, o_ref, bias_ref, key_ref,
                        *, topk):
    tq = q_ref.shape[0]
    hd = pl.program_id(2)

    @pl.when(hd == 0)
    def _():
        ki3 = ki3_ref[...]
        tqv = tailq_ref[...]
        qi = qi_ref[...]
        score = None
        for h in range(IDX_HEADS):
            lg = _dot(_cat3(qi[:, h * IDX_DIM:(h + 1) * IDX_DIM], lhs=True), ki3, NT)
            term = tqv[:, IDX_DIM + h:IDX_DIM + h + 1] * jnp.maximum(lg, 0.0)
            score = term if score is None else score + term
        qpos = pl.program_id(1) * tq + lax.broadcasted_iota(I32, (tq, 1), 0)
        bias_ref[...] = _select_bias(score, qpos, key_ref, topk)

    qs = (q_ref[...] * (HEAD_DIM ** -0.5)).astype(BF16)
    s = _dot(qs, k_ref[...].astype(BF16), NT) + bias_ref[...]
    m = jnp.max(s, axis=1, keepdims=True)
    p = jnp.exp(s - m)
    l = jnp.sum(p, axis=1, keepdims=True)
    o_ref[...] = _dot(p.astype(BF16), v_ref[...].astype(BF16)) / l


def _attn_prompt(qi, tail, ki3, q_hm, k_hm, v_hm, b, t, tq):
    nq = t // tq
    topk = min(TOPK_MAX, t // 4)
    iw = qi.shape[-1]
    hspec_q = pl.BlockSpec((None, None, tq, HEAD_DIM), lambda bi, qb, h: (bi, h, qb, 0))
    hspec_k = pl.BlockSpec((None, None, t, HEAD_DIM), lambda bi, qb, h: (bi, h, 0, 0))
    return pl.pallas_call(
        functools.partial(_attn_prompt_kernel, topk=topk),
        grid=(b, nq, N_HEADS),
        in_specs=[
            pl.BlockSpec((tq, iw), lambda bi, qb, h: (bi * nq + qb, 0)),
            pl.BlockSpec((tq, LANE), lambda bi, qb, h: (bi * nq + qb, 0)),
            pl.BlockSpec((t, 3 * IDX_DIM), lambda bi, qb, h: (bi, 0)),
            hspec_q, hspec_k, hspec_k,
        ],
        out_specs=hspec_q,
        out_shape=jax.ShapeDtypeStruct((b, N_HEADS, t, HEAD_DIM), F32),
        scratch_shapes=[pltpu.VMEM((tq, t), F32), pltpu.VMEM((tq, t), I32)],
        compiler_params=_cparams("arbitrary", "arbitrary", "arbitrary"),
        name="attn_prompt",
    )(qi, tail, ki3, q_hm, k_hm, v_hm)


def _attn_sample_kernel(pt_ref, qi_ref, tail_ref, q_ref, kn_ref, vn_ref, *rest, n_pages, page, topk):
    ki_pages = rest[:n_pages]
    k_pages = rest[n_pages:2 * n_pages]
    v_pages = rest[2 * n_pages:3 * n_pages]
    o_ref, kis_ref, ks_ref, vs_ref, key_ref = rest[3 * n_pages:]
    t = q_ref.shape[0]
    past = n_pages * page
    aw = q_ref.shape[1]
    hsl = lambda h: slice(h * HEAD_DIM, (h + 1) * HEAD_DIM)
    for j in range(n_pages):
        kis_ref[j * page:(j + 1) * page, :] = _cat3(ki_pages[j][...], lhs=False)
        for h in range(N_HEADS):
            rows = pl.ds(h, page, stride=N_HEADS)
            ks_ref[h, j * page:(j + 1) * page, :] = k_pages[j][rows, :].astype(BF16)
            vs_ref[h, j * page:(j + 1) * page, :] = v_pages[j][rows, :].astype(BF16)
    tl = tail_ref[...]
    kis_ref[past:past + page, :] = _cat3(jnp.concatenate(
        [tl[:, :IDX_DIM], jnp.zeros((page - t, IDX_DIM), F32)], 0), lhs=False)
    kn = kn_ref[...]
    vn = vn_ref[...]
    zpad = jnp.zeros((page - t, HEAD_DIM), F32)
    for h in range(N_HEADS):
        ks_ref[h, past:past + page, :] = jnp.concatenate([kn[:, hsl(h)], zpad], 0).astype(BF16)
        vs_ref[h, past:past + page, :] = jnp.concatenate([vn[:, hsl(h)], zpad], 0).astype(BF16)

    qi = qi_ref[...]
    qi_rows = jnp.concatenate([qi[:, h * IDX_DIM:(h + 1) * IDX_DIM] for h in range(IDX_HEADS)], 0)
    w_rows = jnp.concatenate([tl[:, IDX_DIM + h:IDX_DIM + h + 1] for h in range(IDX_HEADS)], 0)
    sc = jnp.maximum(_dot(_cat3(qi_rows, lhs=True), kis_ref[...], NT), 0.0) * w_rows
    score = sc[0:t]
    for h in range(1, IDX_HEADS):
        score = score + sc[h * t:(h + 1) * t]
    qpos = past + lax.broadcasted_iota(I32, (t, 1), 0)
    bias = _select_bias(score, qpos, key_ref, topk)

    q = q_ref[...] * (HEAD_DIM ** -0.5)
    s_pad = bias.shape[1]
    rpad = -t % 16
    outs = []
    for h in range(N_HEADS):
        qh = jnp.concatenate([q[:, hsl(h)], jnp.zeros((rpad, HEAD_DIM), F32)], 0).astype(BF16)
        s = _dot(qh, ks_ref[h], NT)[0:t] + bias
        m = jnp.max(s, axis=1, keepdims=True)
        p = jnp.exp(s - m)
        l = jnp.sum(p, axis=1, keepdims=True)
        ph = jnp.concatenate([p, jnp.zeros((rpad, s_pad), F32)], 0).astype(BF16)
        outs.append(_dot(ph, vs_ref[h])[0:t] / l)
    o_ref[...] = jnp.concatenate(outs, 1)


def _attn_sample(page_table, qi, tail, q, kn, vn, pool_ki, pool_k, pool_v, layer, b, t):
    n_pages = page_table.shape[1]
    page = pool_ki.shape[2]
    past = n_pages * page
    topk = min(TOPK_MAX, (past + t) // 4)
    aw = q.shape[-1]
    iw = qi.shape[-1]
    s_pad = past + page
    row = lambda w: pl.BlockSpec((t, w), lambda bi, pt: (bi, 0))

    def ki_spec(j):
        return pl.BlockSpec((None, None, page, IDX_DIM), lambda bi, pt: (layer, pt[bi, j], 0, 0))

    def kv_spec(j):
        return pl.BlockSpec((None, None, page * N_HEADS, HEAD_DIM),
                            lambda bi, pt: (layer, pt[bi, j], 0, 0))

    in_specs = [row(iw), row(LANE), row(aw), row(aw), row(aw)]
    in_specs += [ki_spec(j) for j in range(n_pages)]
    in_specs += [kv_spec(j) for j in range(n_pages)]
    in_specs += [kv_spec(j) for j in range(n_pages)]
    grid_spec = pltpu.PrefetchScalarGridSpec(
        num_scalar_prefetch=1,
        grid=(b,),
        in_specs=in_specs,
        out_specs=row(aw),
        scratch_shapes=[pltpu.VMEM((s_pad, 3 * IDX_DIM), BF16),
                        pltpu.VMEM((N_HEADS, s_pad, HEAD_DIM), BF16),
                        pltpu.VMEM((N_HEADS, s_pad, HEAD_DIM), BF16),
                        pltpu.VMEM((t, s_pad), I32)],
    )
    return pl.pallas_call(
        functools.partial(_attn_sample_kernel, n_pages=n_pages, page=page, topk=topk),
        grid_spec=grid_spec,
        out_shape=jax.ShapeDtypeStruct((b * t, aw), F32),
        compiler_params=_cparams("arbitrary"),
        name="attn_sample",
    )(page_table, qi, tail, q, kn, vn, *([pool_ki] * n_pages), *([pool_k] * n_pages),
      *([pool_v] * n_pages))


def _outproj_kernel(x_ref, g_ref, ml_ref, o_ref, beta_ref, wh_ref, wl_ref, lg_ref, lb_ref, y_ref):
    bb, tt, d = x_ref.shape
    lw = ml_ref.shape[-1]
    mo = _rms(o_ref[...], beta_ref[...])
    acc = (_dot3w(ml_ref[...], wh_ref[0:lw, :], wl_ref[0:lw, :])
           + _dot3w(mo, wh_ref[lw:, :], wl_ref[lw:, :]))
    y = ALPHA * x_ref[...] + g_ref[...] * acc.reshape(bb, tt, d)
    y_ref[...] = _layer_norm(y, lg_ref[...], lb_ref[...])


def _out_proj(x, mod, layer, mix_lru, o, beta_attn, w_hi, w_lo, ln_g, ln_b, bb, tt):
    b, t, d = x.shape
    nt = t // tt
    rows = bb * tt
    lw = mix_lru.shape[-1]
    aw = o.shape[-1]
    xspec = pl.BlockSpec((bb, tt, d), lambda bi, ti: (bi, ti, 0))
    vec = lambda w: pl.BlockSpec((1, w), lambda bi, ti: (0, 0))
    wspec = pl.BlockSpec((None, lw + aw, d), lambda bi, ti: (layer, 0, 0))
    return pl.pallas_call(
        _outproj_kernel,
        grid=(b // bb, nt),
        in_specs=[xspec,
                  pl.BlockSpec((None, None, bb, 1, d), lambda bi, ti: (layer, 2, bi, 0, 0)),
                  pl.BlockSpec((rows, lw), lambda bi, ti: (bi * nt + ti, 0)),
                  pl.BlockSpec((rows, aw), lambda bi, ti: (bi * nt + ti, 0)),
                  vec(aw), wspec, wspec, vec(d), vec(d)],
        out_specs=xspec,
        out_shape=jax.ShapeDtypeStruct((b, t, d), F32),
        compiler_params=_cparams("arbitrary", "arbitrary"),
        name="out_proj",
    )(x, mod, mix_lru, o, beta_attn, w_hi, w_lo, ln_g, ln_b)


def _extract_top(cur, n_take, on_take):
    rows = cur.shape[0]
    ridx = lax.broadcasted_iota(I32, cur.shape, 0)
    for r in range(n_take):
        m = jnp.max(cur, axis=0, keepdims=True)
        jmin = jnp.min(jnp.where(cur == m, ridx, rows), axis=0, keepdims=True)
        on_take(r, m)
        cur = jnp.where(ridx == jmin, NEG_INF, cur)
    return cur


def _peer_kernel(x_ref, sc_ref, sh_ref, g_ref, wqh_ref, wql_ref, k3_ref, u_ref, v_ref,
                 lg_ref, lb_ref, y_ref,
                 hb_ref, hl_ref, st_ref, e_ref, sv_ref, tau_ref, acc_ref):
    bb, tt, d = x_ref.shape
    tn = bb * tt
    nk = st_ref.shape[1]
    eb = u_ref.shape[0] // nk
    ei = pl.program_id(2)

    @pl.when(ei == 0)
    def _():
        h2 = (x_ref[...] * (1.0 + sc_ref[...]) + sh_ref[...]).reshape(tn, d)
        hh, hl = _split(h2)
        hb_ref[...] = hh
        hl_ref[...] = hl
        acc_ref[...] = jnp.zeros_like(acc_ref)

        def per_head(h, carry):
            svs = []
            for p in range(2):
                hp = 2 * h + p
                qs = _dot3s(hb_ref[...], hl_ref[...], wqh_ref[hp], wql_ref[hp])
                qh, ql = _split(qs)
                s_t = _dot(k3_ref[hp], jnp.concatenate([qh, ql, qh], 1), NT)
                st_ref[hp] = s_t
                vals = []

                def keep(r, m, p=p, vals=vals):
                    vals.append(m)
                    if p == 1:
                        sv_ref[r:r + 1, :] = m

                _extract_top(s_t, PEER_TOPK, keep)
                svs.append(vals)
            sv0, sv1 = svs
            m0, m1 = sv0[0], sv1[0]
            sv1_all = sv_ref[...]
            sv1_half = sv_ref[0:PEER_TOPK // 2, :]
            cand = jnp.concatenate(
                [sv0[a] + (sv1_all if a == 0 else sv1_half) for a in range(PEER_TOPK)], 0)
            mtot = m0 + m1
            acc = {"z": jnp.zeros_like(m0), "tau": m0}

            def take(r, m):
                acc["z"] = acc["z"] + jnp.exp(m - mtot)
                acc["tau"] = m

            _extract_top(cand, PEER_TOPK, take)
            tau_ref[pl.ds(h, 1), :] = acc["tau"]
            inv_z = 1.0 / acc["z"]
            e_ref[2 * h, :, 0:tn] = jnp.exp(st_ref[2 * h] - m0)
            e_ref[2 * h + 1, :, 0:tn] = jnp.exp(st_ref[2 * h + 1] - m1) * inv_z
            return carry

        lax.fori_loop(0, PEER_HEADS, per_head, 0)

    hb = hb_ref[...]
    ws = []
    for e in range(eb):
        tile = ei * eb + e
        act = _gelu(_dot(u_ref[e * nk:(e + 1) * nk, :], hb, NT))
        gate = jnp.zeros((nk, tn), F32)
        for h in range(PEER_HEADS):
            s0 = st_ref[2 * h, pl.ds(tile, 1), :]
            e0 = e_ref[2 * h, pl.ds(tile, 1), 0:tn]
            ssum = st_ref[2 * h + 1] + s0
            gate = gate + jnp.where(ssum >= tau_ref[h:h + 1, :], e_ref[2 * h + 1, :, 0:tn] * e0, 0.0)
        ws.append((gate * act).astype(BF16))
    acc_ref[...] += _dot(jnp.concatenate(ws, 0), v_ref[...], TN)

    @pl.when(ei == pl.num_programs(2) - 1)
    def _():
        y = ALPHA * x_ref[...] + g_ref[...] * acc_ref[...].reshape(bb, tt, d)
        y_ref[...] = _layer_norm(y, lg_ref[...], lb_ref[...])


def _peer(x, mod, layer, wq_hi, wq_lo, k3, u_bf, v_bf, ln_g, ln_b, bb, tt, eb):
    b, t, d = x.shape
    nt = t // tt
    tn = bb * tt
    nhp = 2 * PEER_HEADS
    dk = wq_hi.shape[-1]
    nk = k3.shape[2]
    xspec = pl.BlockSpec((bb, tt, d), lambda bi, ti, ei: (bi, ti, 0))
    vec = pl.BlockSpec((1, d), lambda bi, ti, ei: (0, 0))

    def mod_spec(j):
        return pl.BlockSpec((None, None, bb, 1, d), lambda bi, ti, ei: (layer, j, bi, 0, 0))

    wq_spec = pl.BlockSpec((None, nhp, d, dk), lambda bi, ti, ei: (layer, 0, 0, 0))
    k_spec = pl.BlockSpec((None, nhp, nk, 3 * dk), lambda bi, ti, ei: (layer, 0, 0, 0))
    tab_spec = pl.BlockSpec((None, eb * nk, d), lambda bi, ti, ei: (layer, ei, 0))
    return pl.pallas_call(
        _peer_kernel,
        grid=(b // bb, nt, nk // eb),
        in_specs=[xspec, mod_spec(4), mod_spec(3), mod_spec(5), wq_spec, wq_spec, k_spec,
                  tab_spec, tab_spec, vec, vec],
        out_specs=xspec,
        out_shape=jax.ShapeDtypeStruct((b, t, d), F32),
        scratch_shapes=[pltpu.VMEM((tn, d), BF16), pltpu.VMEM((tn, d), BF16),
                        pltpu.VMEM((nhp, nk, tn), F32), pltpu.VMEM((nhp, nk, tn + LANE), F32),
                        pltpu.VMEM((PEER_TOPK, tn), F32), pltpu.VMEM((PEER_HEADS, tn), F32),
                        pltpu.VMEM((tn, d), F32)],
        compiler_params=_cparams("arbitrary", "arbitrary", "arbitrary"),
        name="peer",
    )(x, mod, mod, mod, wq_hi, wq_lo, k3, u_bf, v_bf, ln_g, ln_b)


def _block_diag(w):
    l, nb, bw, _ = w.shape
    eye = jnp.eye(nb, dtype=w.dtype)
    return jnp.einsum("lncd,nm->lncmd", w, eye).reshape(l, nb * bw, nb * bw)


def _pick_tile(t, target):
    tt = min(t, target)
    while t % tt:
        tt //= 2
    return tt


def kernel(x_prompt, x_sample, cache_k, cache_v, cache_idx_k, state_conv, state_lru, page_table,
           c_prompt, c_sample, w_ada, b_ada, w_in, conv_w, conv_b, lru_wa, lru_ba, lru_wx, lru_bx,
           lru_lambda, beta_lru, beta_attn, w_out, ln1_g, ln1_b, ln2_g, ln2_b, peer_wq, peer_keys,
           peer_u, peer_v):
    depth = w_ada.shape[0]
    bp, tp, d = x_prompt.shape
    bs, ts, _ = x_sample.shape
    lw = d // 2
    aw = N_HEADS * HEAD_DIM
    page = cache_k.shape[2]
    past = page_table.shape[1] * page
    assert tp >= CONV_W - 1 and ts >= CONV_W - 1 and ts % 8 == 0 and bs % 8 == 0
    assert cache_k.shape[3:] == (N_HEADS, HEAD_DIM) and page >= ts and HEAD_DIM == 64

    in_cols = w_in.shape[-1]
    cols_pad = -(-in_cols // LANE) * LANE
    w_in_hi, w_in_lo = _split(jnp.pad(w_in, ((0, 0), (0, 0), (0, cols_pad - in_cols))))
    w_out_hi, w_out_lo = _split(w_out)
    wa_hi, wa_lo = _split(_block_diag(lru_wa))
    wx_hi, wx_lo = _split(_block_diag(lru_wx))
    nhp = 2 * PEER_HEADS
    dk = peer_wq.shape[-1] // nhp
    wq = peer_wq.reshape(depth, d, nhp, dk).transpose(0, 2, 1, 3)
    wq_hi, wq_lo = _split(wq)
    nk = peer_keys.shape[3]
    pk = peer_keys.transpose(0, 2, 1, 3, 4).reshape(depth, nhp, nk, dk)
    pk_hi, pk_lo = _split(pk)
    pk3 = jnp.concatenate([pk_hi, pk_hi, pk_lo], -1)
    u_bf = peer_u.astype(BF16)
    v_bf = peer_v.astype(BF16)
    eb = 4 if nk % 4 == 0 else 1
    n_pool = cache_k.shape[1]
    pool_k = cache_k.reshape(depth, n_pool, page * N_HEADS, HEAD_DIM)
    pool_v = cache_v.reshape(depth, n_pool, page * N_HEADS, HEAD_DIM)

    nb_all = bp + bs
    nb_pad = -(-nb_all // 8) * 8
    c_all = jnp.pad(jnp.concatenate([c_prompt, c_sample], 0), ((0, nb_pad - nb_all), (0, 0)))
    mod = _ada_mod(c_all, w_ada, b_ada)
    mod_p = mod[:, :, :bp].reshape(depth, 6, bp, 1, d)
    mod_s = mod[:, :, bp:nb_all].reshape(depth, 6, bs, 1, d)

    tt_p = _pick_tile(tp, 512)
    tabs_p = [a.reshape(tp // tt_p, tt_p, LANE) for a in _rope_tables(jnp.arange(tp, dtype=I32))]
    bb_s = _pick_tile(bs, 32)
    tabs_s = [jnp.tile(a, (bb_s, 1)).reshape(1, bb_s * ts, LANE)
              for a in _rope_tables(past + jnp.arange(ts, dtype=I32))]

    tt_lru = _pick_tile(tp, 256)
    tq = _pick_tile(tp, Q_BLOCK)
    tn_p = _pick_tile(tp, 512)
    bb_peer = _pick_tile(bs, max(1, 512 // ts))

    yp, ys = x_prompt, x_sample
    outs = {k: [] for k in ("kp", "vp", "kip", "cp", "hp", "ks", "vs", "kis", "cs", "hs")}
    for l in range(depth):
        lru_w = (conv_w[l], conv_b[l][None], wa_hi[l], wa_lo[l], lru_ba[l][None], wx_hi[l], wx_lo[l],
                 lru_bx[l][None], lru_lambda[l][None], beta_lru[l][None])
        zx, zg, q, k, v, qi, tail, ki, ki3 = _in_proj(yp, mod_p, l, w_in_hi, w_in_lo, tabs_p, 1, tt_p)
        zc = jnp.zeros((bp, CONV_W - 1, lw), F32)
        zh = jnp.zeros((bp, lw), F32)
        mix_lru, nconv, nh = _lru_prompt(zx, zg, zc, zh, lru_w, bp, tp, tt_lru)
        to_hm = lambda a: a.reshape(bp, tp, N_HEADS, HEAD_DIM).transpose(0, 2, 1, 3)
        o_hm = _attn_prompt(qi, tail, ki3, to_hm(q), to_hm(k), to_hm(v), bp, tp, tq)
        o = o_hm.transpose(0, 2, 1, 3).reshape(bp * tp, aw)
        x1 = _out_proj(yp, mod_p, l, mix_lru, o, beta_attn[l][None], w_out_hi, w_out_lo,
                       ln1_g[l][None], ln1_b[l][None], 1, tt_p)
        yp = _peer(x1, mod_p, l, wq_hi, wq_lo, pk3, u_bf, v_bf,
                   ln2_g[l][None], ln2_b[l][None], 1, tn_p, eb)
        outs["kp"].append(k.reshape(bp, tp, N_HEADS, HEAD_DIM))
        outs["vp"].append(v.reshape(bp, tp, N_HEADS, HEAD_DIM))
        outs["kip"].append(ki.reshape(bp, tp, IDX_DIM))
        outs["cp"].append(nconv)
        outs["hp"].append(nh.reshape(bp, lw))
        zx, zg, q, k, v, qi, tail, ki, _ = _in_proj(ys, mod_s, l, w_in_hi, w_in_lo, tabs_s, bb_s, ts)
        tm = lambda a: a.reshape(bs, ts, lw).transpose(1, 0, 2)
        mix_t, nconv_t, nh = _lru_sample(tm(zx), tm(zg), state_conv[l].transpose(1, 0, 2),
                                         state_lru[l], lru_w)
        mix_lru = mix_t.transpose(1, 0, 2).reshape(bs * ts, lw)
        o = _attn_sample(page_table, qi, tail, q, k, v, cache_idx_k, pool_k, pool_v, l, bs, ts)
        x1 = _out_proj(ys, mod_s, l, mix_lru, o, beta_attn[l][None], w_out_hi, w_out_lo,
                       ln1_g[l][None], ln1_b[l][None], bb_s, ts)
        ys = _peer(x1, mod_s, l, wq_hi, wq_lo, pk3, u_bf, v_bf,
                   ln2_g[l][None], ln2_b[l][None], bb_peer, ts, eb)
        outs["ks"].append(k.reshape(bs, ts, N_HEADS, HEAD_DIM))
        outs["vs"].append(v.reshape(bs, ts, N_HEADS, HEAD_DIM))
        outs["kis"].append(ki.reshape(bs, ts, IDX_DIM))
        outs["cs"].append(nconv_t.transpose(1, 0, 2))
        outs["hs"].append(nh)

    st = lambda name: jnp.stack(outs[name])
    return (yp, ys, st("kp"), st("vp"), st("kip"), st("cp"), st("hp"),
            st("ks"), st("vs"), st("kis"), st("cs"), st("hs"))
```

```python
import functools
import math

import jax
import jax.numpy as jnp
from jax import lax
from jax.experimental import pallas as pl
from jax.experimental.pallas import tpu as pltpu

F32 = jnp.float32
BF16 = jnp.bfloat16
I32 = jnp.int32

LRU_C = 8.0
CONV_W = 4
N_HEADS = 8
HEAD_DIM = 64
IDX_HEADS = 4
IDX_DIM = 64
TOPK_MAX = 256
ROPE_THETA = 500000.0
ROPE_FRAC_DIV = 4
Q_BLOCK = 128
PEER_HEADS = 8
PEER_TOPK = 16
DEPTH_NOMINAL = 4
ALPHA = (2 * DEPTH_NOMINAL) ** 0.25
LN_EPS = 1e-5

LANE = 128
VMEM_LIMIT = 56 * 1024 * 1024

NN = (((1,), (0,)), ((), ()))
NT = (((1,), (1,)), ((), ()))
TN = (((0,), (0,)), ((), ()))

NEG_INF = float("-inf")
INT_MIN = -2147483648
KEY_NEG_INF = -2139095041


def _cparams(*sem):
    return pltpu.CompilerParams(dimension_semantics=sem, vmem_limit_bytes=VMEM_LIMIT)


def _split(a):
    hi = a.astype(BF16)
    lo = (a - hi.astype(F32)).astype(BF16)
    return hi, lo


def _dot(a, b, dims=NN):
    return lax.dot_general(a, b, dims, preferred_element_type=F32)


def _dot3s(ah, al, bh, bl, dims=NN):
    return _dot(ah, bh, dims) + (_dot(ah, bl, dims) + _dot(al, bh, dims))


def _dot3w(a, bh, bl, dims=NN):
    ah, al = _split(a)
    return _dot3s(ah, al, bh, bl, dims)


def _cat3(x, lhs):
    hi = x.astype(BF16).astype(F32)
    lo = (x - hi).astype(BF16).astype(F32)
    parts = [hi, hi, lo] if lhs else [hi, lo, hi]
    return jnp.concatenate(parts, 1).astype(BF16)


def _gelu(x):
    return 0.5 * x * (1.0 + lax.erf(x * (1.0 / math.sqrt(2.0))))


def _layer_norm(y, g, b):
    mu = jnp.mean(y, -1, keepdims=True)
    d = y - mu
    var = jnp.mean(d * d, -1, keepdims=True)
    return d * lax.rsqrt(var + LN_EPS) * g + b


def _rms(y, g):
    return y * lax.rsqrt(jnp.mean(y * y, -1, keepdims=True) + LN_EPS) * g


def _order_key(x):
    b = pltpu.bitcast(x, I32)
    return jnp.where(b < 0, b ^ 0x7FFFFFFF, b)


def _ada_kernel(c_ref, w_ref, b_ref, o_ref):
    c = c_ref[...]
    s = c * jax.nn.sigmoid(c)
    wh, wl = _split(w_ref[...])
    o_ref[...] = _dot3w(s, wh, wl) + b_ref[...]


def _ada_mod(c_all, w_ada, b_ada):
    depth, d, six_d = w_ada.shape
    nb = c_all.shape[0]
    return pl.pallas_call(
        _ada_kernel,
        grid=(depth, six_d // d),
        in_specs=[
            pl.BlockSpec((nb, d), lambda l, j: (0, 0)),
            pl.BlockSpec((None, d, d), lambda l, j: (l, 0, j)),
            pl.BlockSpec((None, None, 1, d), lambda l, j: (l, j, 0, 0)),
        ],
        out_specs=pl.BlockSpec((None, None, nb, d), lambda l, j: (l, j, 0, 0)),
        out_shape=jax.ShapeDtypeStruct((depth, six_d // d, nb, d), F32),
        compiler_params=_cparams("arbitrary", "arbitrary"),
        name="ada_mod",
    )(c_all, w_ada, b_ada.reshape(depth, six_d // d, 1, d))


def _rope(x, c, s1, s2):
    w = x.shape[-1]
    return x * c + pltpu.roll(x, 8, 1) * s1 + pltpu.roll(x, w - 8, 1) * s2


def _inproj_kernel(x_ref, sc_ref, sh_ref, wh_ref, wl_ref, rc_ref, rs1_ref, rs2_ref,
                   tc_ref, ts1_ref, ts2_ref,
                   zx_ref, zg_ref, q_ref, k_ref, v_ref, qi_ref, tail_ref, ki_ref, ki3_ref,
                   kb_ref, vb_ref):
    bb, tt, d = x_ref.shape
    h = x_ref[...] * (1.0 + sc_ref[...]) + sh_ref[...]
    hh, hl = _split(h.reshape(bb * tt, d))

    def proj(lo, hi):
        return _dot3s(hh, hl, wh_ref[:, lo:hi], wl_ref[:, lo:hi])

    lw = zx_ref.shape[-1]
    aw = q_ref.shape[-1]
    iw = qi_ref.shape[-1]
    o = 0
    zx_ref[...] = proj(o, o + lw); o += lw
    zg_ref[...] = proj(o, o + lw); o += lw
    rc, rs1, rs2 = rc_ref[...], rs1_ref[...], rs2_ref[...]
    rep = aw // LANE
    c4 = jnp.concatenate([rc] * rep, 1)
    s14 = jnp.concatenate([rs1] * rep, 1)
    s24 = jnp.concatenate([rs2] * rep, 1)
    q_ref[...] = _rope(proj(o, o + aw), c4, s14, s24); o += aw
    kk = _rope(proj(o, o + aw), c4, s14, s24); o += aw
    k_ref[...] = kk
    kb_ref[...] = kk.astype(BF16)
    vv = proj(o, o + aw); o += aw
    v_ref[...] = vv
    vb_ref[...] = vv.astype(BF16)
    repi = iw // LANE
    qi_ref[...] = _rope(proj(o, o + iw), jnp.concatenate([rc] * repi, 1),
                        jnp.concatenate([rs1] * repi, 1), jnp.concatenate([rs2] * repi, 1)); o += iw
    tail = _rope(proj(o, o + LANE), tc_ref[...], ts1_ref[...], ts2_ref[...])
    tail_ref[...] = tail
    ki_ref[...] = tail[:, :IDX_DIM]
    ki3_ref[...] = _cat3(tail[:, :IDX_DIM], lhs=False)


def _rope_tables(pos):
    t = pos.shape[0]
    rot = HEAD_DIM // ROPE_FRAC_DIV
    half = rot // 2
    freqs = ROPE_THETA ** (-jnp.arange(half, dtype=F32) / half)
    ang = pos.astype(F32)[:, None] * freqs[None, :]
    cos, sin = jnp.cos(ang), jnp.sin(ang)
    ones = jnp.ones((t, HEAD_DIM - rot), F32)
    zeros = jnp.zeros((t, HEAD_DIM - rot), F32)
    zh = jnp.zeros((t, half), F32)
    c64 = jnp.concatenate([cos, cos, ones], 1)
    s1_64 = jnp.concatenate([zh, sin, zeros], 1)
    s2_64 = jnp.concatenate([-sin, zh, zeros], 1)
    rc = jnp.concatenate([c64, c64], 1)
    rs1 = jnp.concatenate([s1_64, s1_64], 1)
    rs2 = jnp.concatenate([s2_64, s2_64], 1)
    wscale = jnp.concatenate([jnp.full((t, IDX_HEADS), IDX_HEADS ** -0.5, F32),
                              jnp.ones((t, LANE - IDX_DIM - IDX_HEADS), F32)], 1)
    z64 = jnp.zeros((t, LANE - IDX_DIM), F32)
    tc = jnp.concatenate([c64, wscale], 1)
    ts1 = jnp.concatenate([s1_64, z64], 1)
    ts2 = jnp.concatenate([s2_64, z64], 1)
    return rc, rs1, rs2, tc, ts1, ts2


def _in_proj(x, mod, layer, w_hi, w_lo, tables, bb, tt):
    b, t, d = x.shape
    n = b * t
    rows = bb * tt
    nt = t // tt
    lw = d // 2
    aw = N_HEADS * HEAD_DIM
    iw = IDX_HEADS * IDX_DIM
    cols = w_hi.shape[-1]
    tab_idx = (lambda bi, ti: (ti, 0, 0)) if tables[0].shape[0] > 1 else (lambda bi, ti: (0, 0, 0))
    tab_spec = pl.BlockSpec((None, rows, LANE), tab_idx)
    row_idx = lambda bi, ti: (bi * nt + ti, 0)

    def mod_spec(j):
        return pl.BlockSpec((None, None, bb, 1, d), lambda bi, ti: (layer, j, bi, 0, 0))

    outs = [(n, lw), (n, lw), (n, aw), (n, aw), (n, aw), (n, iw), (n, LANE), (n, IDX_DIM),
            (n, 3 * IDX_DIM), (n, aw), (n, aw)]
    dtypes = [F32] * 8 + [BF16] * 3
    return pl.pallas_call(
        _inproj_kernel,
        grid=(b // bb, nt),
        in_specs=[
            pl.BlockSpec((bb, tt, d), lambda bi, ti: (bi, ti, 0)),
            mod_spec(1), mod_spec(0),
            pl.BlockSpec((None, d, cols), lambda bi, ti: (layer, 0, 0)),
            pl.BlockSpec((None, d, cols), lambda bi, ti: (layer, 0, 0)),
        ] + [tab_spec] * 6,
        out_specs=[pl.BlockSpec((rows, w), row_idx) for _, w in outs],
        out_shape=[jax.ShapeDtypeStruct(s, dt) for s, dt in zip(outs, dtypes)],
        compiler_params=_cparams("arbitrary", "arbitrary"),
        name="in_proj",
    )(x, mod, mod, w_hi, w_lo, *tables)


def _lru_gates(xc, wah, wal, ba, wxh, wxl, bx, lam):
    xh, xl = _split(xc)
    r = jax.nn.sigmoid(_dot3s(xh, xl, wah, wal) + ba)
    i = jax.nn.sigmoid(_dot3s(xh, xl, wxh, wxl) + bx)
    nl = -lam
    sp = jnp.maximum(nl, 0.0) + jnp.log1p(jnp.exp(-jnp.abs(nl)))
    log_a = -LRU_C * r * sp
    a = jnp.exp(log_a)
    u = jnp.sqrt(1.0 - a * a) * (i * xc)
    return a, u


def _lru_prompt_kernel(zx_ref, zg_ref, cbuf_ref, h0_ref, cw_ref, cb_ref, wah_ref, wal_ref, ba_ref,
                       wxh_ref, wxl_ref, bx_ref, lam_ref, beta_ref,
                       mix_ref, nconv_ref, nh_ref, ext_ref, hst_ref):
    tt = zx_ref.shape[0]
    ti = pl.program_id(1)

    @pl.when(ti == 0)
    def _():
        ext_ref[5:8, :] = cbuf_ref[...]
        hst_ref[0:1, :] = h0_ref[...]

    x = zx_ref[...]
    ext_ref[8:8 + tt, :] = x
    w = cw_ref[...]
    xc = (cb_ref[...] + ext_ref[5:5 + tt, :] * w[0:1] + ext_ref[6:6 + tt, :] * w[1:2]
          + ext_ref[7:7 + tt, :] * w[2:3] + x * w[3:4])
    tail3 = ext_ref[tt + 5:tt + 8, :]
    ext_ref[5:8, :] = tail3

    a, u = _lru_gates(xc, wah_ref[...], wal_ref[...], ba_ref[...], wxh_ref[...], wxl_ref[...],
                      bx_ref[...], lam_ref[...])
    row = lax.broadcasted_iota(I32, (tt, 1), 0)
    s = 1
    while s < tt:
        a_sh = pltpu.roll(a, s, 0)
        u_sh = pltpu.roll(u, s, 0)
        m = row >= s
        u = u + a * jnp.where(m, u_sh, 0.0)
        a = a * jnp.where(m, a_sh, 1.0)
        s *= 2
    h = a * hst_ref[0:1, :] + u
    hst_ref[0:1, :] = h[tt - 1:tt, :]
    y = _gelu(zg_ref[...]) * h
    mix_ref[...] = _rms(y, beta_ref[...])

    @pl.when(ti == pl.num_programs(1) - 1)
    def _():
        nconv_ref[...] = tail3
        nh_ref[...] = h[tt - 1:tt, :]


def _lru_prompt(zx, zg, cbuf, h0, lw, b, t, tt):
    n, w = zx.shape
    nt = t // tt
    row_spec = pl.BlockSpec((tt, w), lambda bi, ti: (bi * nt + ti, 0))
    full = lambda shape: pl.BlockSpec(shape, lambda bi, ti: tuple(0 for _ in shape))
    vec = full((1, w))
    mat = full((w, w))
    return pl.pallas_call(
        _lru_prompt_kernel,
        grid=(b, nt),
        in_specs=[row_spec, row_spec,
                  pl.BlockSpec((None, CONV_W - 1, w), lambda bi, ti: (bi, 0, 0)),
                  pl.BlockSpec((None, 1, w), lambda bi, ti: (bi, 0, 0)),
                  full((CONV_W, w)), vec, mat, mat, vec, mat, mat, vec, vec, vec],
        out_specs=[row_spec,
                   pl.BlockSpec((None, CONV_W - 1, w), lambda bi, ti: (bi, 0, 0)),
                   pl.BlockSpec((None, 1, w), lambda bi, ti: (bi, 0, 0))],
        out_shape=[jax.ShapeDtypeStruct((n, w), F32),
                   jax.ShapeDtypeStruct((b, CONV_W - 1, w), F32),
                   jax.ShapeDtypeStruct((b, 1, w), F32)],
        scratch_shapes=[pltpu.VMEM((tt + 8, w), F32), pltpu.VMEM((8, w), F32)],
        compiler_params=_cparams("arbitrary", "arbitrary"),
        name="lru_prompt",
    )(zx, zg, cbuf, h0.reshape(b, 1, w), *lw)


def _lru_sample_kernel(zx_ref, zg_ref, cbuf_ref, h0_ref, cw_ref, cb_ref, wah_ref, wal_ref, ba_ref,
                       wxh_ref, wxl_ref, bx_ref, lam_ref, beta_ref,
                       mix_ref, nconv_ref, nh_ref, xc_ref):
    t, b, wd = zx_ref.shape
    w = cw_ref[...]
    xp = [cbuf_ref[k] for k in range(CONV_W - 1)] + [zx_ref[k] for k in range(t)]
    for k in range(t):
        xc_ref[k * b:(k + 1) * b, :] = (cb_ref[...] + xp[k] * w[0:1] + xp[k + 1] * w[1:2]
                                        + xp[k + 2] * w[2:3] + xp[k + 3] * w[3:4])
    for k in range(CONV_W - 1):
        nconv_ref[k] = xp[t + k]
    a, u = _lru_gates(xc_ref[...], wah_ref[...], wal_ref[...], ba_ref[...], wxh_ref[...],
                      wxl_ref[...], bx_ref[...], lam_ref[...])
    h = h0_ref[...]
    beta = beta_ref[...]
    for k in range(t):
        h = a[k * b:(k + 1) * b, :] * h + u[k * b:(k + 1) * b, :]
        mix_ref[k] = _rms(_gelu(zg_ref[k]) * h, beta)
    nh_ref[...] = h


def _lru_sample(zx_t, zg_t, cbuf_t, h0, lw):
    t, b, w = zx_t.shape
    full = lambda shape: pl.BlockSpec(shape, lambda i: tuple(0 for _ in shape))
    vec = full((1, w))
    mat = full((w, w))
    return pl.pallas_call(
        _lru_sample_kernel,
        grid=(1,),
        in_specs=[full((t, b, w)), full((t, b, w)), full((CONV_W - 1, b, w)), full((b, w)),
                  full((CONV_W, w)), vec, mat, mat, vec, mat, mat, vec, vec, vec],
        out_specs=[full((t, b, w)), full((CONV_W - 1, b, w)), full((b, w))],
        out_shape=[jax.ShapeDtypeStruct((t, b, w), F32),
                   jax.ShapeDtypeStruct((CONV_W - 1, b, w), F32),
                   jax.ShapeDtypeStruct((b, w), F32)],
        scratch_shapes=[pltpu.VMEM((t * b, w), F32)],
        compiler_params=_cparams("arbitrary"),
        name="lru_sample",
    )(zx_t, zg_t, cbuf_t, h0, *lw)


def _select_bias(score, qpos, key_ref, topk):
    q, s = score.shape
    kpos = lax.broadcasted_iota(I32, (q, s), 1)
    vis = kpos <= qpos
    key_ref[...] = jnp.where(vis, _order_key(score + 0.0), KEY_NEG_INF)

    kf = float(topk)

    def count_ge(t):
        return jnp.sum(jnp.where(key_ref[...] >= t, 1.0, 0.0), axis=1, keepdims=True)

    c0 = count_ge(jnp.zeros((q, 1), I32))
    small = (qpos + 1) <= topk
    t0 = jnp.where(small, KEY_NEG_INF + 1, jnp.where(c0 >= kf, 0, INT_MIN)).astype(I32)
    r0 = jnp.where(small, 1, jnp.where(c0 == kf, 1, 0)).astype(I32)

    def vcond(c):
        i, _, res = c
        return jnp.logical_and(i < 31, jnp.min(res) == 0)

    def vbody(c):
        i, t, res = c
        cand = t | jnp.left_shift(jnp.int32(1), 30 - i)
        cnt = count_ge(cand)
        t = jnp.where(res > 0, t, jnp.where(cnt >= kf, cand, t))
        res = jnp.where(cnt == kf, 1, res)
        return i + 1, t, res

    _, thr, res = lax.while_loop(vcond, vbody, (jnp.int32(0), t0, r0))
    nbits = max(1, (s - 1).bit_length())

    def tie_phase():
        c_gt = jnp.sum(jnp.where(key_ref[...] > thr, 1.0, 0.0), axis=1, keepdims=True)
        need = kf - c_gt

        def ibody(i, j):
            cand = j + jnp.left_shift(jnp.int32(1), nbits - 1 - i)
            kp = lax.broadcasted_iota(I32, (q, s), 1)
            hit = jnp.where(key_ref[...] == thr, jnp.where(kp <= cand, 1.0, 0.0), 0.0)
            f = jnp.sum(hit, axis=1, keepdims=True)
            return jnp.where(f < need, cand, j)

        return lax.fori_loop(0, nbits, ibody, jnp.full((q, 1), -1, I32)) + 1

    jsel = lax.cond(jnp.min(res) == 0, tie_phase, lambda: jnp.full((q, 1), s, I32))
    jsel = jnp.where(res > 0, s, jsel)
    key = key_ref[...]
    sel = jnp.where(key > thr, 1, jnp.where(key == thr, jnp.where(kpos <= jsel, 1, 0), 0))
    sel = jnp.where(vis, sel, 0)
    return jnp.where(sel > 0, 0.0, NEG_INF)


def _attn_prompt_kernel(qi_ref, tailq_ref, ki3_ref, q_ref, k_ref, v_ref, o_ref, bias_ref, key_ref,
                        *, topk):
    tq = q_ref.shape[0]
    ki3 = ki3_ref[...]
    tqv = tailq_ref[...]
    qi = qi_ref[...]
    score = None
    for h in range(IDX_HEADS):
        lg = _dot(_cat3(qi[:, h * IDX_DIM:(h + 1) * IDX_DIM], lhs=True), ki3, NT)
        term = tqv[:, IDX_DIM + h:IDX_DIM + h + 1] * jnp.maximum(lg, 0.0)
        score = term if score is None else score + term
    qpos = pl.program_id(1) * tq + lax.broadcasted_iota(I32, (tq, 1), 0)
    bias_ref[0:tq, :] = _select_bias(score, qpos, key_ref, topk)
    bias_ref[tq:2 * tq, :] = bias_ref[0:tq, :]

    low = lax.broadcasted_iota(I32, (tq, LANE), 1) < HEAD_DIM
    for g in range(q_ref.shape[1] // LANE):
        cs = slice(g * LANE, (g + 1) * LANE)
        qg = q_ref[:, cs] * (HEAD_DIM ** -0.5)
        qpair = jnp.concatenate([jnp.where(low, qg, 0.0), jnp.where(low, 0.0, qg)], 0).astype(BF16)
        s = _dot(qpair, k_ref[:, cs], NT) + bias_ref[...]
        m = jnp.max(s, axis=1, keepdims=True)
        p = jnp.exp(s - m)
        l = jnp.sum(p, axis=1, keepdims=True)
        og = _dot(p.astype(BF16), v_ref[:, cs]) / l
        o_ref[:, cs] = jnp.where(low, og[0:tq], og[tq:2 * tq])


def _attn_prompt(qi, tail, ki3, q, k_bf, v_bf, b, t, tq):
    nq = t // tq
    topk = min(TOPK_MAX, t // 4)
    iw = qi.shape[-1]
    aw = q.shape[-1]
    qrow = lambda w: pl.BlockSpec((tq, w), lambda bi, qb: (bi * nq + qb, 0))
    kall = lambda w: pl.BlockSpec((t, w), lambda bi, qb: (bi, 0))
    return pl.pallas_call(
        functools.partial(_attn_prompt_kernel, topk=topk),
        grid=(b, nq),
        in_specs=[qrow(iw), qrow(LANE), kall(3 * IDX_DIM), qrow(aw), kall(aw), kall(aw)],
        out_specs=qrow(aw),
        out_shape=jax.ShapeDtypeStruct((b * t, aw), F32),
        scratch_shapes=[pltpu.VMEM((2 * tq, t), F32), pltpu.VMEM((tq, t), I32)],
        compiler_params=_cparams("arbitrary", "arbitrary"),
        name="attn_prompt",
    )(qi, tail, ki3, q, k_bf, v_bf)


def _attn_sample_kernel(pt_ref, qi_ref, tail_ref, q_ref, kn_ref, vn_ref, *rest, n_pages, page, topk):
    ki_pages = rest[:n_pages]
    k_pages = rest[n_pages:2 * n_pages]
    v_pages = rest[2 * n_pages:3 * n_pages]
    o_ref, kit_ref, kst_ref, vst_ref, key_ref = rest[3 * n_pages:]
    t = q_ref.shape[0]
    past = n_pages * page
    hsl = lambda h: slice(h * HEAD_DIM, (h + 1) * HEAD_DIM)
    for j in range(n_pages):
        cols = slice(j * page, (j + 1) * page)
        kit = ki_pages[j][...]
        hi = kit.astype(BF16)
        lo = (kit - hi.astype(F32)).astype(BF16)
        kit_ref[0:IDX_DIM, cols] = hi
        kit_ref[IDX_DIM:2 * IDX_DIM, cols] = lo
        kit_ref[2 * IDX_DIM:3 * IDX_DIM, cols] = hi
        kst_ref[:, :, cols] = k_pages[j][...].astype(BF16)
        vst_ref[:, :, cols] = v_pages[j][...].astype(BF16)
    tl = tail_ref[...]
    ki_new = _cat3(jnp.concatenate([tl[:, :IDX_DIM], jnp.zeros((page - t, IDX_DIM), F32)], 0),
                   lhs=False)
    kn = kn_ref[...]
    vn = vn_ref[...]
    zpad = jnp.zeros((page - t, HEAD_DIM), F32)

    qi = qi_ref[...]
    qi_rows = jnp.concatenate([qi[:, h * IDX_DIM:(h + 1) * IDX_DIM] for h in range(IDX_HEADS)], 0)
    w_rows = jnp.concatenate([tl[:, IDX_DIM + h:IDX_DIM + h + 1] for h in range(IDX_HEADS)], 0)
    qi3 = _cat3(qi_rows, lhs=True)
    logits = jnp.concatenate([_dot(qi3, kit_ref[...]), _dot(qi3, ki_new, NT)], 1)
    sc = jnp.maximum(logits, 0.0) * w_rows
    score = sc[0:t]
    for h in range(1, IDX_HEADS):
        score = score + sc[h * t:(h + 1) * t]
    qpos = past + lax.broadcasted_iota(I32, (t, 1), 0)
    bias = _select_bias(score, qpos, key_ref, topk)

    q = q_ref[...] * (HEAD_DIM ** -0.5)
    rpad = -t % 16
    outs = []
    for h in range(N_HEADS):
        qh = jnp.concatenate([q[:, hsl(h)], jnp.zeros((rpad, HEAD_DIM), F32)], 0).astype(BF16)
        k_new = jnp.concatenate([kn[:, hsl(h)], zpad], 0).astype(BF16)
        v_new = jnp.concatenate([vn[:, hsl(h)], zpad], 0).astype(BF16)
        s = jnp.concatenate([_dot(qh, kst_ref[h]), _dot(qh, k_new, NT)], 1)[0:t] + bias
        m = jnp.max(s, axis=1, keepdims=True)
        p = jnp.exp(s - m)
        l = jnp.sum(p, axis=1, keepdims=True)
        ph = jnp.concatenate([p, jnp.zeros((rpad, past + page), F32)], 0).astype(BF16)
        o = _dot(ph[:, :past], vst_ref[h], NT) + _dot(ph[:, past:], v_new)
        outs.append(o[0:t] / l)
    o_ref[...] = jnp.concatenate(outs, 1)


def _attn_sample(page_table, qi, tail, q, kn, vn, pool_ki, pool_k, pool_v, layer, b, t):
    n_pages = page_table.shape[1]
    page = pool_ki.shape[3]
    past = n_pages * page
    topk = min(TOPK_MAX, (past + t) // 4)
    aw = q.shape[-1]
    iw = qi.shape[-1]
    s_pad = past + page
    row = lambda w: pl.BlockSpec((t, w), lambda bi, pt: (bi, 0))

    def ki_spec(j):
        return pl.BlockSpec((None, None, IDX_DIM, page), lambda bi, pt: (layer, pt[bi, j], 0, 0))

    def kv_spec(j):
        return pl.BlockSpec((None, None, N_HEADS, HEAD_DIM, page),
                            lambda bi, pt: (layer, pt[bi, j], 0, 0, 0))

    in_specs = [row(iw), row(LANE), row(aw), row(aw), row(aw)]
    in_specs += [ki_spec(j) for j in range(n_pages)]
    in_specs += [kv_spec(j) for j in range(n_pages)]
    in_specs += [kv_spec(j) for j in range(n_pages)]
    grid_spec = pltpu.PrefetchScalarGridSpec(
        num_scalar_prefetch=1,
        grid=(b,),
        in_specs=in_specs,
        out_specs=row(aw),
        scratch_shapes=[pltpu.VMEM((3 * IDX_DIM, past), BF16),
                        pltpu.VMEM((N_HEADS, HEAD_DIM, past), BF16),
                        pltpu.VMEM((N_HEADS, HEAD_DIM, past), BF16),
                        pltpu.VMEM((t, s_pad), I32)],
    )
    return pl.pallas_call(
        functools.partial(_attn_sample_kernel, n_pages=n_pages, page=page, topk=topk),
        grid_spec=grid_spec,
        out_shape=jax.ShapeDtypeStruct((b * t, aw), F32),
        compiler_params=_cparams("arbitrary"),
        name="attn_sample",
    )(page_table, qi, tail, q, kn, vn, *([pool_ki] * n_pages), *([pool_k] * n_pages),
      *([pool_v] * n_pages))


def _outproj_kernel(x_ref, g_ref, ml_ref, o_ref, beta_ref, wh_ref, wl_ref, lg_ref, lb_ref, y_ref):
    bb, tt, d = x_ref.shape
    lw = ml_ref.shape[-1]
    mo = _rms(o_ref[...], beta_ref[...])
    acc = (_dot3w(ml_ref[...], wh_ref[0:lw, :], wl_ref[0:lw, :])
           + _dot3w(mo, wh_ref[lw:, :], wl_ref[lw:, :]))
    y = ALPHA * x_ref[...] + g_ref[...] * acc.reshape(bb, tt, d)
    y_ref[...] = _layer_norm(y, lg_ref[...], lb_ref[...])


def _out_proj(x, mod, layer, mix_lru, o, beta_attn, w_hi, w_lo, ln_g, ln_b, bb, tt):
    b, t, d = x.shape
    nt = t // tt
    rows = bb * tt
    lw = mix_lru.shape[-1]
    aw = o.shape[-1]
    xspec = pl.BlockSpec((bb, tt, d), lambda bi, ti: (bi, ti, 0))
    vec = lambda w: pl.BlockSpec((1, w), lambda bi, ti: (0, 0))
    wspec = pl.BlockSpec((None, lw + aw, d), lambda bi, ti: (layer, 0, 0))
    return pl.pallas_call(
        _outproj_kernel,
        grid=(b // bb, nt),
        in_specs=[xspec,
                  pl.BlockSpec((None, None, bb, 1, d), lambda bi, ti: (layer, 2, bi, 0, 0)),
                  pl.BlockSpec((rows, lw), lambda bi, ti: (bi * nt + ti, 0)),
                  pl.BlockSpec((rows, aw), lambda bi, ti: (bi * nt + ti, 0)),
                  vec(aw), wspec, wspec, vec(d), vec(d)],
        out_specs=xspec,
        out_shape=jax.ShapeDtypeStruct((b, t, d), F32),
        compiler_params=_cparams("arbitrary", "arbitrary"),
        name="out_proj",
    )(x, mod, mix_lru, o, beta_attn, w_hi, w_lo, ln_g, ln_b)


def _extract_top(cur, n_take, on_take):
    rows = cur.shape[0]
    ridx = lax.broadcasted_iota(I32, cur.shape, 0)
    for r in range(n_take):
        m = jnp.max(cur, axis=0, keepdims=True)
        jmin = jnp.min(jnp.where(cur == m, ridx, rows), axis=0, keepdims=True)
        on_take(r, m)
        cur = jnp.where(ridx == jmin, NEG_INF, cur)
    return cur


def _peer_kernel(x_ref, sc_ref, sh_ref, g_ref, wqh_ref, wql_ref, k3_ref, u_ref, v_ref,
                 lg_ref, lb_ref, y_ref,
                 hb_ref, hl_ref, st_ref, e_ref, sv_ref, tau_ref, acc_ref):
    bb, tt, d = x_ref.shape
    tn = bb * tt
    nk = st_ref.shape[1]
    eb = u_ref.shape[0] // nk
    ei = pl.program_id(2)

    @pl.when(ei == 0)
    def _():
        h2 = (x_ref[...] * (1.0 + sc_ref[...]) + sh_ref[...]).reshape(tn, d)
        hh, hl = _split(h2)
        hb_ref[...] = hh
        hl_ref[...] = hl
        acc_ref[...] = jnp.zeros_like(acc_ref)

        def per_head(h, slot):
            svs = []
            for p in range(2):
                hp = 2 * h + p
                qs = _dot3s(hb_ref[...], hl_ref[...], wqh_ref[hp], wql_ref[hp])
                qh, ql = _split(qs)
                s_t = _dot(k3_ref[hp], jnp.concatenate([qh, ql, qh], 1), NT)
                st_ref[hp] = s_t
                vals = []

                def keep(r, m, p=p, vals=vals):
                    vals.append(m)
                    if p == 1:
                        sv_ref[slot, r:r + 1, :] = m

                _extract_top(s_t, PEER_TOPK, keep)
                svs.append(vals)
            sv0, sv1 = svs
            m0, m1 = sv0[0], sv1[0]
            sv1_all = sv_ref[slot]
            sv1_half = sv_ref[slot, 0:PEER_TOPK // 2, :]
            cand = jnp.concatenate(
                [sv0[a] + (sv1_all if a == 0 else sv1_half) for a in range(PEER_TOPK)], 0)
            mtot = m0 + m1
            acc = {"z": jnp.zeros_like(m0), "tau": m0}

            def take(r, m):
                acc["z"] = acc["z"] + jnp.exp(m - mtot)
                acc["tau"] = m

            _extract_top(cand, PEER_TOPK, take)
            tau_ref[pl.ds(h, 1), :] = acc["tau"]
            inv_z = 1.0 / acc["z"]
            e_ref[2 * h, :, 0:tn] = jnp.exp(st_ref[2 * h] - m0)
            e_ref[2 * h + 1, :, 0:tn] = jnp.exp(st_ref[2 * h + 1] - m1) * inv_z

        def head_pair(hh, carry):
            per_head(2 * hh, 0)
            per_head(2 * hh + 1, 1)
            return carry

        lax.fori_loop(0, PEER_HEADS // 2, head_pair, 0)

    hb = hb_ref[...]
    acts = [_gelu(_dot(u_ref[e * nk:(e + 1) * nk, :], hb, NT)) for e in range(eb)]
    group = 2 if eb % 2 == 0 else 1
    cw = LANE if tn % LANE == 0 else tn
    ws = [[None] * (tn // cw) for _ in range(eb)]
    for e0i in range(0, eb, group):
        tiles = [ei * eb + e0i + k for k in range(group)]
        for c in range(tn // cw):
            cs = slice(c * cw, (c + 1) * cw)
            gates = [jnp.zeros((nk, cw), F32) for _ in range(group)]
            for h in range(PEER_HEADS):
                s1 = st_ref[2 * h + 1, :, cs]
                e1 = e_ref[2 * h + 1, :, cs]
                tau = tau_ref[h:h + 1, cs]
                for k in range(group):
                    s0 = st_ref[2 * h, pl.ds(tiles[k], 1), :][:, cs]
                    e0 = e_ref[2 * h, pl.ds(tiles[k], 1), 0:tn][:, cs]
                    gates[k] = gates[k] + jnp.where(s1 + s0 >= tau, e1 * e0, 0.0)
            for k in range(group):
                ws[e0i + k][c] = (gates[k] * acts[e0i + k][:, cs]).astype(BF16)
    w_all = jnp.concatenate([jnp.concatenate(row, 1) for row in ws], 0)
    acc_ref[...] += _dot(w_all, v_ref[...], TN)

    @pl.when(ei == pl.num_programs(2) - 1)
    def _():
        y = ALPHA * x_ref[...] + g_ref[...] * acc_ref[...].reshape(bb, tt, d)
        y_ref[...] = _layer_norm(y, lg_ref[...], lb_ref[...])


def _peer(x, mod, layer, wq_hi, wq_lo, k3, u_bf, v_bf, ln_g, ln_b, bb, tt, eb):
    b, t, d = x.shape
    nt = t // tt
    tn = bb * tt
    nhp = 2 * PEER_HEADS
    dk = wq_hi.shape[-1]
    nk = k3.shape[2]
    xspec = pl.BlockSpec((bb, tt, d), lambda bi, ti, ei: (bi, ti, 0))
    vec = pl.BlockSpec((1, d), lambda bi, ti, ei: (0, 0))

    def mod_spec(j):
        return pl.BlockSpec((None, None, bb, 1, d), lambda bi, ti, ei: (layer, j, bi, 0, 0))

    wq_spec = pl.BlockSpec((None, nhp, d, dk), lambda bi, ti, ei: (layer, 0, 0, 0))
    k_spec = pl.BlockSpec((None, nhp, nk, 3 * dk), lambda bi, ti, ei: (layer, 0, 0, 0))
    tab_spec = pl.BlockSpec((None, eb * nk, d), lambda bi, ti, ei: (layer, ei, 0))
    return pl.pallas_call(
        _peer_kernel,
        grid=(b // bb, nt, nk // eb),
        in_specs=[xspec, mod_spec(4), mod_spec(3), mod_spec(5), wq_spec, wq_spec, k_spec,
                  tab_spec, tab_spec, vec, vec],
        out_specs=xspec,
        out_shape=jax.ShapeDtypeStruct((b, t, d), F32),
        scratch_shapes=[pltpu.VMEM((tn, d), BF16), pltpu.VMEM((tn, d), BF16),
                        pltpu.VMEM((nhp, nk, tn), F32), pltpu.VMEM((nhp, nk, tn + LANE), F32),
                        pltpu.VMEM((2, PEER_TOPK, tn), F32), pltpu.VMEM((PEER_HEADS, tn), F32),
                        pltpu.VMEM((tn, d), F32)],
        compiler_params=_cparams("arbitrary", "arbitrary", "arbitrary"),
        name="peer",
    )(x, mod, mod, mod, wq_hi, wq_lo, k3, u_bf, v_bf, ln_g, ln_b)


def _block_diag(w):
    l, nb, bw, _ = w.shape
    eye = jnp.eye(nb, dtype=w.dtype)
    return jnp.einsum("lncd,nm->lncmd", w, eye).reshape(l, nb * bw, nb * bw)


def _pick_tile(t, target):
    tt = min(t, target)
    while t % tt:
        tt //= 2
    return tt


def kernel(x_prompt, x_sample, cache_k, cache_v, cache_idx_k, state_conv, state_lru, page_table,
           c_prompt, c_sample, w_ada, b_ada, w_in, conv_w, conv_b, lru_wa, lru_ba, lru_wx, lru_bx,
           lru_lambda, beta_lru, beta_attn, w_out, ln1_g, ln1_b, ln2_g, ln2_b, peer_wq, peer_keys,
           peer_u, peer_v):
    depth = w_ada.shape[0]
    bp, tp, d = x_prompt.shape
    bs, ts, _ = x_sample.shape
    lw = d // 2
    aw = N_HEADS * HEAD_DIM
    page = cache_k.shape[2]
    past = page_table.shape[1] * page
    assert tp >= CONV_W - 1 and ts >= CONV_W - 1 and ts % 8 == 0 and bs % 8 == 0
    assert cache_k.shape[3:] == (N_HEADS, HEAD_DIM) and page >= ts and HEAD_DIM == 64

    in_cols = w_in.shape[-1]
    cols_pad = -(-in_cols // LANE) * LANE
    w_in_hi, w_in_lo = _split(jnp.pad(w_in, ((0, 0), (0, 0), (0, cols_pad - in_cols))))
    w_out_hi, w_out_lo = _split(w_out)
    wa_hi, wa_lo = _split(_block_diag(lru_wa))
    wx_hi, wx_lo = _split(_block_diag(lru_wx))
    nhp = 2 * PEER_HEADS
    dk = peer_wq.shape[-1] // nhp
    wq = peer_wq.reshape(depth, d, nhp, dk).transpose(0, 2, 1, 3)
    wq_hi, wq_lo = _split(wq)
    nk = peer_keys.shape[3]
    pk = peer_keys.transpose(0, 2, 1, 3, 4).reshape(depth, nhp, nk, dk)
    pk_hi, pk_lo = _split(pk)
    pk3 = jnp.concatenate([pk_hi, pk_hi, pk_lo], -1)
    u_bf = peer_u.astype(BF16)
    v_bf = peer_v.astype(BF16)
    eb = 4 if nk % 4 == 0 else 1
    pool_k = cache_k.transpose(0, 1, 3, 4, 2)
    pool_v = cache_v.transpose(0, 1, 3, 4, 2)
    pool_ki = cache_idx_k.transpose(0, 1, 3, 2)

    nb_all = bp + bs
    nb_pad = -(-nb_all // 8) * 8
    c_all = jnp.pad(jnp.concatenate([c_prompt, c_sample], 0), ((0, nb_pad - nb_all), (0, 0)))
    mod = _ada_mod(c_all, w_ada, b_ada)
    mod_p = mod[:, :, :bp].reshape(depth, 6, bp, 1, d)
    mod_s = mod[:, :, bp:nb_all].reshape(depth, 6, bs, 1, d)

    tt_p = _pick_tile(tp, 512)
    tabs_p = [a.reshape(tp // tt_p, tt_p, LANE) for a in _rope_tables(jnp.arange(tp, dtype=I32))]
    bb_s = _pick_tile(bs, 32)
    tabs_s = [jnp.tile(a, (bb_s, 1)).reshape(1, bb_s * ts, LANE)
              for a in _rope_tables(past + jnp.arange(ts, dtype=I32))]

    tt_lru = _pick_tile(tp, 256)
    tq = _pick_tile(tp, Q_BLOCK)
    tn_p = _pick_tile(tp, 512)
    bb_peer = _pick_tile(bs, max(1, 512 // ts))

    yp, ys = x_prompt, x_sample
    outs = {k: [] for k in ("kp", "vp", "kip", "cp", "hp", "ks", "vs", "kis", "cs", "hs")}
    for l in range(depth):
        lru_w = (conv_w[l], conv_b[l][None], wa_hi[l], wa_lo[l], lru_ba[l][None], wx_hi[l], wx_lo[l],
                 lru_bx[l][None], lru_lambda[l][None], beta_lru[l][None])
        zx, zg, q, k, v, qi, tail, ki, ki3, k16, v16 = _in_proj(yp, mod_p, l, w_in_hi, w_in_lo,
                                                                tabs_p, 1, tt_p)
        zc = jnp.zeros((bp, CONV_W - 1, lw), F32)
        zh = jnp.zeros((bp, lw), F32)
        mix_lru, nconv, nh = _lru_prompt(zx, zg, zc, zh, lru_w, bp, tp, tt_lru)
        o = _attn_prompt(qi, tail, ki3, q, k16, v16, bp, tp, tq)
        x1 = _out_proj(yp, mod_p, l, mix_lru, o, beta_attn[l][None], w_out_hi, w_out_lo,
                       ln1_g[l][None], ln1_b[l][None], 1, tt_p)
        yp = _peer(x1, mod_p, l, wq_hi, wq_lo, pk3, u_bf, v_bf,
                   ln2_g[l][None], ln2_b[l][None], 1, tn_p, eb)
        outs["kp"].append(k.reshape(bp, tp, N_HEADS, HEAD_DIM))
        outs["vp"].append(v.reshape(bp, tp, N_HEADS, HEAD_DIM))
        outs["kip"].append(ki.reshape(bp, tp, IDX_DIM))
        outs["cp"].append(nconv)
        outs["hp"].append(nh.reshape(bp, lw))
        zx, zg, q, k, v, qi, tail, ki = _in_proj(ys, mod_s, l, w_in_hi, w_in_lo, tabs_s, bb_s, ts)[:8]
        tm = lambda a: a.reshape(bs, ts, lw).transpose(1, 0, 2)
        mix_t, nconv_t, nh = _lru_sample(tm(zx), tm(zg), state_conv[l].transpose(1, 0, 2),
                                         state_lru[l], lru_w)
        mix_lru = mix_t.transpose(1, 0, 2).reshape(bs * ts, lw)
        o = _attn_sample(page_table, qi, tail, q, k, v, pool_ki, pool_k, pool_v, l, bs, ts)
        x1 = _out_proj(ys, mod_s, l, mix_lru, o, beta_attn[l][None], w_out_hi, w_out_lo,
                       ln1_g[l][None], ln1_b[l][None], bb_s, ts)
        ys = _peer(x1, mod_s, l, wq_hi, wq_lo, pk3, u_bf, v_bf,
                   ln2_g[l][None], ln2_b[l][None], bb_peer, ts, eb)
        outs["ks"].append(k.reshape(bs, ts, N_HEADS, HEAD_DIM))
        outs["vs"].append(v.reshape(bs, ts, N_HEADS, HEAD_DIM))
        outs["kis"].append(ki.reshape(bs, ts, IDX_DIM))
        outs["cs"].append(nconv_t.transpose(1, 0, 2))
        outs["hs"].append(nh)

    st = lambda name: jnp.stack(outs[name])
    return (yp, ys, st("kp"), st("vp"), st("kip"), st("cp"), st("hp"),
            st("ks"), st("vs"), st("kis"), st("cs"), st("hs"))
```

```python
import functools
import math

import jax
import jax.numpy as jnp
from jax import lax
from jax.experimental import pallas as pl
from jax.experimental.pallas import tpu as pltpu

F32 = jnp.float32
BF16 = jnp.bfloat16
I32 = jnp.int32

LRU_C = 8.0
CONV_W = 4
N_HEADS = 8
HEAD_DIM = 64
IDX_HEADS = 4
IDX_DIM = 64
TOPK_MAX = 256
ROPE_THETA = 500000.0
ROPE_FRAC_DIV = 4
Q_BLOCK = 128
PEER_HEADS = 8
PEER_TOPK = 16
DEPTH_NOMINAL = 4
ALPHA = (2 * DEPTH_NOMINAL) ** 0.25
LN_EPS = 1e-5

LANE = 128
VMEM_LIMIT = 56 * 1024 * 1024

NN = (((1,), (0,)), ((), ()))
NT = (((1,), (1,)), ((), ()))
TN = (((0,), (0,)), ((), ()))

NEG_INF = float("-inf")
INT_MIN = -2147483648
KEY_NEG_INF = -2139095041


def _cparams(*sem):
    return pltpu.CompilerParams(dimension_semantics=sem, vmem_limit_bytes=VMEM_LIMIT)


def _split(a):
    hi = a.astype(BF16)
    lo = (a - hi.astype(F32)).astype(BF16)
    return hi, lo


def _dot(a, b, dims=NN):
    return lax.dot_general(a, b, dims, preferred_element_type=F32)


def _dot3s(ah, al, bh, bl, dims=NN):
    return _dot(ah, bh, dims) + (_dot(ah, bl, dims) + _dot(al, bh, dims))


def _dot3w(a, bh, bl, dims=NN):
    ah, al = _split(a)
    return _dot3s(ah, al, bh, bl, dims)


def _cat3(x, lhs):
    hi = x.astype(BF16).astype(F32)
    lo = (x - hi).astype(BF16).astype(F32)
    parts = [hi, hi, lo] if lhs else [hi, lo, hi]
    return jnp.concatenate(parts, 1).astype(BF16)


def _gelu(x):
    return 0.5 * x * (1.0 + lax.erf(x * (1.0 / math.sqrt(2.0))))


def _layer_norm(y, g, b):
    mu = jnp.mean(y, -1, keepdims=True)
    d = y - mu
    var = jnp.mean(d * d, -1, keepdims=True)
    return d * lax.rsqrt(var + LN_EPS) * g + b


def _rms(y, g):
    return y * lax.rsqrt(jnp.mean(y * y, -1, keepdims=True) + LN_EPS) * g


def _order_key(x):
    b = pltpu.bitcast(x, I32)
    return jnp.where(b < 0, b ^ 0x7FFFFFFF, b)


def _ada_kernel(c_ref, w_ref, b_ref, o_ref):
    c = c_ref[...]
    s = c * jax.nn.sigmoid(c)
    wh, wl = _split(w_ref[...])
    o_ref[...] = _dot3w(s, wh, wl) + b_ref[...]


def _ada_mod(c_all, w_ada, b_ada):
    depth, d, six_d = w_ada.shape
    nb = c_all.shape[0]
    return pl.pallas_call(
        _ada_kernel,
        grid=(depth, six_d // d),
        in_specs=[
            pl.BlockSpec((nb, d), lambda l, j: (0, 0)),
            pl.BlockSpec((None, d, d), lambda l, j: (l, 0, j)),
            pl.BlockSpec((None, None, 1, d), lambda l, j: (l, j, 0, 0)),
        ],
        out_specs=pl.BlockSpec((None, None, nb, d), lambda l, j: (l, j, 0, 0)),
        out_shape=jax.ShapeDtypeStruct((depth, six_d // d, nb, d), F32),
        compiler_params=_cparams("arbitrary", "arbitrary"),
        name="ada_mod",
    )(c_all, w_ada, b_ada.reshape(depth, six_d // d, 1, d))


def _rope(x, c, s1, s2):
    w = x.shape[-1]
    return x * c + pltpu.roll(x, 8, 1) * s1 + pltpu.roll(x, w - 8, 1) * s2


def _inproj_kernel(x_ref, sc_ref, sh_ref, wh_ref, wl_ref, rc_ref, rs1_ref, rs2_ref,
                   tc_ref, ts1_ref, ts2_ref,
                   zx_ref, zg_ref, q_ref, k_ref, v_ref, qi_ref, tail_ref, ki_ref, ki3_ref,
                   kb_ref, vb_ref):
    bb, tt, d = x_ref.shape
    h = x_ref[...] * (1.0 + sc_ref[...]) + sh_ref[...]
    hh, hl = _split(h.reshape(bb * tt, d))

    def proj(lo, hi):
        return _dot3s(hh, hl, wh_ref[:, lo:hi], wl_ref[:, lo:hi])

    lw = zx_ref.shape[-1]
    aw = q_ref.shape[-1]
    iw = qi_ref.shape[-1]
    o = 0
    zx_ref[...] = proj(o, o + lw); o += lw
    zg_ref[...] = proj(o, o + lw); o += lw
    rc, rs1, rs2 = rc_ref[...], rs1_ref[...], rs2_ref[...]
    rep = aw // LANE
    c4 = jnp.concatenate([rc] * rep, 1)
    s14 = jnp.concatenate([rs1] * rep, 1)
    s24 = jnp.concatenate([rs2] * rep, 1)
    q_ref[...] = _rope(proj(o, o + aw), c4, s14, s24); o += aw
    kk = _rope(proj(o, o + aw), c4, s14, s24); o += aw
    k_ref[...] = kk
    kb_ref[...] = kk.astype(BF16)
    vv = proj(o, o + aw); o += aw
    v_ref[...] = vv
    vb_ref[...] = vv.astype(BF16)
    repi = iw // LANE
    qi_ref[...] = _rope(proj(o, o + iw), jnp.concatenate([rc] * repi, 1),
                        jnp.concatenate([rs1] * repi, 1), jnp.concatenate([rs2] * repi, 1)); o += iw
    tail = _rope(proj(o, o + LANE), tc_ref[...], ts1_ref[...], ts2_ref[...])
    tail_ref[...] = tail
    ki_ref[...] = tail[:, :IDX_DIM]
    ki3_ref[...] = _cat3(tail[:, :IDX_DIM], lhs=False)


def _rope_tables(pos):
    t = pos.shape[0]
    rot = HEAD_DIM // ROPE_FRAC_DIV
    half = rot // 2
    freqs = ROPE_THETA ** (-jnp.arange(half, dtype=F32) / half)
    ang = pos.astype(F32)[:, None] * freqs[None, :]
    cos, sin = jnp.cos(ang), jnp.sin(ang)
    ones = jnp.ones((t, HEAD_DIM - rot), F32)
    zeros = jnp.zeros((t, HEAD_DIM - rot), F32)
    zh = jnp.zeros((t, half), F32)
    c64 = jnp.concatenate([cos, cos, ones], 1)
    s1_64 = jnp.concatenate([zh, sin, zeros], 1)
    s2_64 = jnp.concatenate([-sin, zh, zeros], 1)
    rc = jnp.concatenate([c64, c64], 1)
    rs1 = jnp.concatenate([s1_64, s1_64], 1)
    rs2 = jnp.concatenate([s2_64, s2_64], 1)
    wscale = jnp.concatenate([jnp.full((t, IDX_HEADS), IDX_HEADS ** -0.5, F32),
                              jnp.ones((t, LANE - IDX_DIM - IDX_HEADS), F32)], 1)
    z64 = jnp.zeros((t, LANE - IDX_DIM), F32)
    tc = jnp.concatenate([c64, wscale], 1)
    ts1 = jnp.concatenate([s1_64, z64], 1)
    ts2 = jnp.concatenate([s2_64, z64], 1)
    return rc, rs1, rs2, tc, ts1, ts2


def _in_proj(x, mod, layer, w_hi, w_lo, tables, bb, tt):
    b, t, d = x.shape
    n = b * t
    rows = bb * tt
    nt = t // tt
    lw = d // 2
    aw = N_HEADS * HEAD_DIM
    iw = IDX_HEADS * IDX_DIM
    cols = w_hi.shape[-1]
    tab_idx = (lambda bi, ti: (ti, 0, 0)) if tables[0].shape[0] > 1 else (lambda bi, ti: (0, 0, 0))
    tab_spec = pl.BlockSpec((None, rows, LANE), tab_idx)
    row_idx = lambda bi, ti: (bi * nt + ti, 0)

    def mod_spec(j):
        return pl.BlockSpec((None, None, bb, 1, d), lambda bi, ti: (layer, j, bi, 0, 0))

    outs = [(n, lw), (n, lw), (n, aw), (n, aw), (n, aw), (n, iw), (n, LANE), (n, IDX_DIM),
            (n, 3 * IDX_DIM), (n, aw), (n, aw)]
    dtypes = [F32] * 8 + [BF16] * 3
    return pl.pallas_call(
        _inproj_kernel,
        grid=(b // bb, nt),
        in_specs=[
            pl.BlockSpec((bb, tt, d), lambda bi, ti: (bi, ti, 0)),
            mod_spec(1), mod_spec(0),
            pl.BlockSpec((None, d, cols), lambda bi, ti: (layer, 0, 0)),
            pl.BlockSpec((None, d, cols), lambda bi, ti: (layer, 0, 0)),
        ] + [tab_spec] * 6,
        out_specs=[pl.BlockSpec((rows, w), row_idx) for _, w in outs],
        out_shape=[jax.ShapeDtypeStruct(s, dt) for s, dt in zip(outs, dtypes)],
        compiler_params=_cparams("arbitrary", "arbitrary"),
        name="in_proj",
    )(x, mod, mod, w_hi, w_lo, *tables)


def _lru_gates(xc, wah, wal, ba, wxh, wxl, bx, lam):
    xh, xl = _split(xc)
    r = jax.nn.sigmoid(_dot3s(xh, xl, wah, wal) + ba)
    i = jax.nn.sigmoid(_dot3s(xh, xl, wxh, wxl) + bx)
    nl = -lam
    sp = jnp.maximum(nl, 0.0) + jnp.log1p(jnp.exp(-jnp.abs(nl)))
    log_a = -LRU_C * r * sp
    a = jnp.exp(log_a)
    u = jnp.sqrt(1.0 - a * a) * (i * xc)
    return a, u


def _lru_prompt_kernel(zx_ref, zg_ref, cbuf_ref, h0_ref, cw_ref, cb_ref, wah_ref, wal_ref, ba_ref,
                       wxh_ref, wxl_ref, bx_ref, lam_ref, beta_ref,
                       mix_ref, nconv_ref, nh_ref, ext_ref, hst_ref):
    tt = zx_ref.shape[0]
    ti = pl.program_id(1)

    @pl.when(ti == 0)
    def _():
        ext_ref[5:8, :] = cbuf_ref[...]
        hst_ref[0:1, :] = h0_ref[...]

    x = zx_ref[...]
    ext_ref[8:8 + tt, :] = x
    w = cw_ref[...]
    xc = (cb_ref[...] + ext_ref[5:5 + tt, :] * w[0:1] + ext_ref[6:6 + tt, :] * w[1:2]
          + ext_ref[7:7 + tt, :] * w[2:3] + x * w[3:4])
    tail3 = ext_ref[tt + 5:tt + 8, :]
    ext_ref[5:8, :] = tail3

    a, u = _lru_gates(xc, wah_ref[...], wal_ref[...], ba_ref[...], wxh_ref[...], wxl_ref[...],
                      bx_ref[...], lam_ref[...])
    row = lax.broadcasted_iota(I32, (tt, 1), 0)
    s = 1
    while s < tt:
        a_sh = pltpu.roll(a, s, 0)
        u_sh = pltpu.roll(u, s, 0)
        m = row >= s
        u = u + a * jnp.where(m, u_sh, 0.0)
        a = a * jnp.where(m, a_sh, 1.0)
        s *= 2
    h = a * hst_ref[0:1, :] + u
    hst_ref[0:1, :] = h[tt - 1:tt, :]
    y = _gelu(zg_ref[...]) * h
    mix_ref[...] = _rms(y, beta_ref[...])

    @pl.when(ti == pl.num_programs(1) - 1)
    def _():
        nconv_ref[...] = tail3
        nh_ref[...] = h[tt - 1:tt, :]


def _lru_prompt(zx, zg, cbuf, h0, lw, b, t, tt):
    n, w = zx.shape
    nt = t // tt
    row_spec = pl.BlockSpec((tt, w), lambda bi, ti: (bi * nt + ti, 0))
    full = lambda shape: pl.BlockSpec(shape, lambda bi, ti: tuple(0 for _ in shape))
    vec = full((1, w))
    mat = full((w, w))
    return pl.pallas_call(
        _lru_prompt_kernel,
        grid=(b, nt),
        in_specs=[row_spec, row_spec,
                  pl.BlockSpec((None, CONV_W - 1, w), lambda bi, ti: (bi, 0, 0)),
                  pl.BlockSpec((None, 1, w), lambda bi, ti: (bi, 0, 0)),
                  full((CONV_W, w)), vec, mat, mat, vec, mat, mat, vec, vec, vec],
        out_specs=[row_spec,
                   pl.BlockSpec((None, CONV_W - 1, w), lambda bi, ti: (bi, 0, 0)),
                   pl.BlockSpec((None, 1, w), lambda bi, ti: (bi, 0, 0))],
        out_shape=[jax.ShapeDtypeStruct((n, w), F32),
                   jax.ShapeDtypeStruct((b, CONV_W - 1, w), F32),
                   jax.ShapeDtypeStruct((b, 1, w), F32)],
        scratch_shapes=[pltpu.VMEM((tt + 8, w), F32), pltpu.VMEM((8, w), F32)],
        compiler_params=_cparams("arbitrary", "arbitrary"),
        name="lru_prompt",
    )(zx, zg, cbuf, h0.reshape(b, 1, w), *lw)


def _lru_sample_kernel(zx_ref, zg_ref, cbuf_ref, h0_ref, cw_ref, cb_ref, wah_ref, wal_ref, ba_ref,
                       wxh_ref, wxl_ref, bx_ref, lam_ref, beta_ref,
                       mix_ref, nconv_ref, nh_ref, xc_ref):
    t, b, wd = zx_ref.shape
    w = cw_ref[...]
    xp = [cbuf_ref[k] for k in range(CONV_W - 1)] + [zx_ref[k] for k in range(t)]
    for k in range(t):
        xc_ref[k * b:(k + 1) * b, :] = (cb_ref[...] + xp[k] * w[0:1] + xp[k + 1] * w[1:2]
                                        + xp[k + 2] * w[2:3] + xp[k + 3] * w[3:4])
    for k in range(CONV_W - 1):
        nconv_ref[k] = xp[t + k]
    a, u = _lru_gates(xc_ref[...], wah_ref[...], wal_ref[...], ba_ref[...], wxh_ref[...],
                      wxl_ref[...], bx_ref[...], lam_ref[...])
    h = h0_ref[...]
    beta = beta_ref[...]
    for k in range(t):
        h = a[k * b:(k + 1) * b, :] * h + u[k * b:(k + 1) * b, :]
        mix_ref[k] = _rms(_gelu(zg_ref[k]) * h, beta)
    nh_ref[...] = h


def _lru_sample(zx_t, zg_t, cbuf_t, h0, lw):
    t, b, w = zx_t.shape
    full = lambda shape: pl.BlockSpec(shape, lambda i: tuple(0 for _ in shape))
    vec = full((1, w))
    mat = full((w, w))
    return pl.pallas_call(
        _lru_sample_kernel,
        grid=(1,),
        in_specs=[full((t, b, w)), full((t, b, w)), full((CONV_W - 1, b, w)), full((b, w)),
                  full((CONV_W, w)), vec, mat, mat, vec, mat, mat, vec, vec, vec],
        out_specs=[full((t, b, w)), full((CONV_W - 1, b, w)), full((b, w))],
        out_shape=[jax.ShapeDtypeStruct((t, b, w), F32),
                   jax.ShapeDtypeStruct((CONV_W - 1, b, w), F32),
                   jax.ShapeDtypeStruct((b, w), F32)],
        scratch_shapes=[pltpu.VMEM((t * b, w), F32)],
        compiler_params=_cparams("arbitrary"),
        name="lru_sample",
    )(zx_t, zg_t, cbuf_t, h0, *lw)


def _select_bias(score, qpos, key_ref, topk):
    q, s = score.shape
    kpos = lax.broadcasted_iota(I32, (q, s), 1)
    vis = kpos <= qpos
    key_ref[...] = jnp.where(vis, _order_key(score + 0.0), KEY_NEG_INF)

    kf = float(topk)

    def count_ge(t):
        return jnp.sum(jnp.where(key_ref[...] >= t, 1.0, 0.0), axis=1, keepdims=True)

    c0 = count_ge(jnp.zeros((q, 1), I32))
    small = (qpos + 1) <= topk
    t0 = jnp.where(small, KEY_NEG_INF + 1, jnp.where(c0 >= kf, 0, INT_MIN)).astype(I32)
    r0 = jnp.where(small, 1, jnp.where(c0 == kf, 1, 0)).astype(I32)

    def vcond(c):
        i, _, res = c
        return jnp.logical_and(i < 31, jnp.min(res) == 0)

    def vbody(c):
        i, t, res = c
        cand = t | jnp.left_shift(jnp.int32(1), 30 - i)
        cnt = count_ge(cand)
        t = jnp.where(res > 0, t, jnp.where(cnt >= kf, cand, t))
        res = jnp.where(cnt == kf, 1, res)
        return i + 1, t, res

    _, thr, res = lax.while_loop(vcond, vbody, (jnp.int32(0), t0, r0))
    nbits = max(1, (s - 1).bit_length())

    def tie_phase():
        c_gt = jnp.sum(jnp.where(key_ref[...] > thr, 1.0, 0.0), axis=1, keepdims=True)
        need = kf - c_gt

        def ibody(i, j):
            cand = j + jnp.left_shift(jnp.int32(1), nbits - 1 - i)
            kp = lax.broadcasted_iota(I32, (q, s), 1)
            hit = jnp.where(key_ref[...] == thr, jnp.where(kp <= cand, 1.0, 0.0), 0.0)
            f = jnp.sum(hit, axis=1, keepdims=True)
            return jnp.where(f < need, cand, j)

        return lax.fori_loop(0, nbits, ibody, jnp.full((q, 1), -1, I32)) + 1

    jsel = lax.cond(jnp.min(res) == 0, tie_phase, lambda: jnp.full((q, 1), s, I32))
    jsel = jnp.where(res > 0, s, jsel)
    key = key_ref[...]
    sel = jnp.where(key > thr, 1, jnp.where(key == thr, jnp.where(kpos <= jsel, 1, 0), 0))
    sel = jnp.where(vis, sel, 0)
    return jnp.where(sel > 0, 0.0, NEG_INF)


def _attn_prompt_kernel(qi_ref, tailq_ref, ki3_ref, q_ref, k_ref, v_ref, o_ref, bias_ref, key_ref,
                        *, topk, qb0):
    tq = q_ref.shape[0]
    ki3 = ki3_ref[...]
    tqv = tailq_ref[...]
    qi = qi_ref[...]
    score = None
    for h in range(IDX_HEADS):
        lg = _dot(_cat3(qi[:, h * IDX_DIM:(h + 1) * IDX_DIM], lhs=True), ki3, NT)
        term = tqv[:, IDX_DIM + h:IDX_DIM + h + 1] * jnp.maximum(lg, 0.0)
        score = term if score is None else score + term
    qpos = (qb0 + pl.program_id(1)) * tq + lax.broadcasted_iota(I32, (tq, 1), 0)
    bias_ref[0:tq, :] = _select_bias(score, qpos, key_ref, topk)
    bias_ref[tq:2 * tq, :] = bias_ref[0:tq, :]

    low = lax.broadcasted_iota(I32, (tq, LANE), 1) < HEAD_DIM
    for g in range(q_ref.shape[1] // LANE):
        cs = slice(g * LANE, (g + 1) * LANE)
        qg = q_ref[:, cs] * (HEAD_DIM ** -0.5)
        qpair = jnp.concatenate([jnp.where(low, qg, 0.0), jnp.where(low, 0.0, qg)], 0).astype(BF16)
        s = _dot(qpair, k_ref[:, cs], NT) + bias_ref[...]
        m = jnp.max(s, axis=1, keepdims=True)
        p = jnp.exp(s - m)
        l = jnp.sum(p, axis=1, keepdims=True)
        og = _dot(p.astype(BF16), v_ref[:, cs]) / l
        o_ref[:, cs] = jnp.where(low, og[0:tq], og[tq:2 * tq])


def _attn_prompt(qi, tail, ki3, q, k_bf, v_bf, b, t, tq, n_ranges):
    nq = t // tq
    topk = min(TOPK_MAX, t // 4)
    iw = qi.shape[-1]
    aw = q.shape[-1]
    nqr = nq // n_ranges
    keys3 = lambda a: a.reshape(b, t, a.shape[-1])
    outs = []
    for r in range(n_ranges):
        qb0 = r * nqr
        tv = (r + 1) * nqr * tq
        qrow = lambda w, qb0=qb0: pl.BlockSpec((tq, w), lambda bi, qb: (bi * nq + qb0 + qb, 0))
        kvis = lambda w, tv=tv: pl.BlockSpec((None, tv, w), lambda bi, qb: (bi, 0, 0))
        outs.append(pl.pallas_call(
            functools.partial(_attn_prompt_kernel, topk=topk, qb0=qb0),
            grid=(b, nqr),
            in_specs=[qrow(iw), qrow(LANE), kvis(3 * IDX_DIM), qrow(aw), kvis(aw), kvis(aw)],
            out_specs=pl.BlockSpec((None, tq, aw), lambda bi, qb: (bi, qb, 0)),
            out_shape=jax.ShapeDtypeStruct((b, nqr * tq, aw), F32),
            scratch_shapes=[pltpu.VMEM((2 * tq, tv), F32), pltpu.VMEM((tq, tv), I32)],
            compiler_params=_cparams("arbitrary", "arbitrary"),
            name="attn_prompt",
        )(qi, tail, keys3(ki3), q, keys3(k_bf), keys3(v_bf)))
    return jnp.concatenate(outs, 1).reshape(b * t, aw)


def _attn_sample_kernel(pt_ref, qi_ref, tail_ref, q_ref, kn_ref, vn_ref, *rest, n_pages, page, topk):
    ki_pages = rest[:n_pages]
    k_pages = rest[n_pages:2 * n_pages]
    v_pages = rest[2 * n_pages:3 * n_pages]
    o_ref, kit_ref, kst_ref, vst_ref, key_ref = rest[3 * n_pages:]
    t = q_ref.shape[0]
    past = n_pages * page
    hsl = lambda h: slice(h * HEAD_DIM, (h + 1) * HEAD_DIM)
    for j in range(n_pages):
        cols = slice(j * page, (j + 1) * page)
        kit = ki_pages[j][...]
        hi = kit.astype(BF16)
        lo = (kit - hi.astype(F32)).astype(BF16)
        kit_ref[0:IDX_DIM, cols] = hi
        kit_ref[IDX_DIM:2 * IDX_DIM, cols] = lo
        kit_ref[2 * IDX_DIM:3 * IDX_DIM, cols] = hi
        kst_ref[:, :, cols] = k_pages[j][...].astype(BF16)
        vst_ref[:, :, cols] = v_pages[j][...].astype(BF16)
    tl = tail_ref[...]
    ki_new = _cat3(jnp.concatenate([tl[:, :IDX_DIM], jnp.zeros((page - t, IDX_DIM), F32)], 0),
                   lhs=False)
    kn = kn_ref[...]
    vn = vn_ref[...]
    zpad = jnp.zeros((page - t, HEAD_DIM), F32)

    qi = qi_ref[...]
    qi_rows = jnp.concatenate([qi[:, h * IDX_DIM:(h + 1) * IDX_DIM] for h in range(IDX_HEADS)], 0)
    w_rows = jnp.concatenate([tl[:, IDX_DIM + h:IDX_DIM + h + 1] for h in range(IDX_HEADS)], 0)
    qi3 = _cat3(qi_rows, lhs=True)
    logits = jnp.concatenate([_dot(qi3, kit_ref[...]), _dot(qi3, ki_new, NT)], 1)
    sc = jnp.maximum(logits, 0.0) * w_rows
    score = sc[0:t]
    for h in range(1, IDX_HEADS):
        score = score + sc[h * t:(h + 1) * t]
    qpos = past + lax.broadcasted_iota(I32, (t, 1), 0)
    bias = _select_bias(score, qpos, key_ref, topk)

    q = q_ref[...] * (HEAD_DIM ** -0.5)
    rpad = -t % 16
    outs = []
    for h in range(N_HEADS):
        qh = jnp.concatenate([q[:, hsl(h)], jnp.zeros((rpad, HEAD_DIM), F32)], 0).astype(BF16)
        k_new = jnp.concatenate([kn[:, hsl(h)], zpad], 0).astype(BF16)
        v_new = jnp.concatenate([vn[:, hsl(h)], zpad], 0).astype(BF16)
        s = jnp.concatenate([_dot(qh, kst_ref[h]), _dot(qh, k_new, NT)], 1)[0:t] + bias
        m = jnp.max(s, axis=1, keepdims=True)
        p = jnp.exp(s - m)
        l = jnp.sum(p, axis=1, keepdims=True)
        ph = jnp.concatenate([p, jnp.zeros((rpad, past + page), F32)], 0).astype(BF16)
        o = _dot(ph[:, :past], vst_ref[h], NT) + _dot(ph[:, past:], v_new)
        outs.append(o[0:t] / l)
    o_ref[...] = jnp.concatenate(outs, 1)


def _attn_sample(page_table, qi, tail, q, kn, vn, pool_ki, pool_k, pool_v, layer, b, t):
    n_pages = page_table.shape[1]
    page = pool_ki.shape[3]
    past = n_pages * page
    topk = min(TOPK_MAX, (past + t) // 4)
    aw = q.shape[-1]
    iw = qi.shape[-1]
    s_pad = past + page
    row = lambda w: pl.BlockSpec((t, w), lambda bi, pt: (bi, 0))

    def ki_spec(j):
        return pl.BlockSpec((None, None, IDX_DIM, page), lambda bi, pt: (layer, pt[bi, j], 0, 0))

    def kv_spec(j):
        return pl.BlockSpec((None, None, N_HEADS, HEAD_DIM, page),
                            lambda bi, pt: (layer, pt[bi, j], 0, 0, 0))

    in_specs = [row(iw), row(LANE), row(aw), row(aw), row(aw)]
    in_specs += [ki_spec(j) for j in range(n_pages)]
    in_specs += [kv_spec(j) for j in range(n_pages)]
    in_specs += [kv_spec(j) for j in range(n_pages)]
    grid_spec = pltpu.PrefetchScalarGridSpec(
        num_scalar_prefetch=1,
        grid=(b,),
        in_specs=in_specs,
        out_specs=row(aw),
        scratch_shapes=[pltpu.VMEM((3 * IDX_DIM, past), BF16),
                        pltpu.VMEM((N_HEADS, HEAD_DIM, past), BF16),
                        pltpu.VMEM((N_HEADS, HEAD_DIM, past), BF16),
                        pltpu.VMEM((t, s_pad), I32)],
    )
    return pl.pallas_call(
        functools.partial(_attn_sample_kernel, n_pages=n_pages, page=page, topk=topk),
        grid_spec=grid_spec,
        out_shape=jax.ShapeDtypeStruct((b * t, aw), F32),
        compiler_params=_cparams("arbitrary"),
        name="attn_sample",
    )(page_table, qi, tail, q, kn, vn, *([pool_ki] * n_pages), *([pool_k] * n_pages),
      *([pool_v] * n_pages))


def _outproj_kernel(x_ref, g_ref, ml_ref, o_ref, beta_ref, wh_ref, wl_ref, lg_ref, lb_ref, y_ref):
    bb, tt, d = x_ref.shape
    lw = ml_ref.shape[-1]
    mo = _rms(o_ref[...], beta_ref[...])
    acc = (_dot3w(ml_ref[...], wh_ref[0:lw, :], wl_ref[0:lw, :])
           + _dot3w(mo, wh_ref[lw:, :], wl_ref[lw:, :]))
    y = ALPHA * x_ref[...] + g_ref[...] * acc.reshape(bb, tt, d)
    y_ref[...] = _layer_norm(y, lg_ref[...], lb_ref[...])


def _out_proj(x, mod, layer, mix_lru, o, beta_attn, w_hi, w_lo, ln_g, ln_b, bb, tt):
    b, t, d = x.shape
    nt = t // tt
    rows = bb * tt
    lw = mix_lru.shape[-1]
    aw = o.shape[-1]
    xspec = pl.BlockSpec((bb, tt, d), lambda bi, ti: (bi, ti, 0))
    vec = lambda w: pl.BlockSpec((1, w), lambda bi, ti: (0, 0))
    wspec = pl.BlockSpec((None, lw + aw, d), lambda bi, ti: (layer, 0, 0))
    return pl.pallas_call(
        _outproj_kernel,
        grid=(b // bb, nt),
        in_specs=[xspec,
                  pl.BlockSpec((None, None, bb, 1, d), lambda bi, ti: (layer, 2, bi, 0, 0)),
                  pl.BlockSpec((rows, lw), lambda bi, ti: (bi * nt + ti, 0)),
                  pl.BlockSpec((rows, aw), lambda bi, ti: (bi * nt + ti, 0)),
                  vec(aw), wspec, wspec, vec(d), vec(d)],
        out_specs=xspec,
        out_shape=jax.ShapeDtypeStruct((b, t, d), F32),
        compiler_params=_cparams("arbitrary", "arbitrary"),
        name="out_proj",
    )(x, mod, mix_lru, o, beta_attn, w_hi, w_lo, ln_g, ln_b)


SUBLANES = 8


def _bitonic_merge(seq):
    seq = list(seq)
    n = len(seq)
    d = n // 2
    while d >= 1:
        for i in range(n):
            if i & d == 0:
                a, b = seq[i], seq[i + d]
                seq[i], seq[i + d] = jnp.maximum(a, b), jnp.minimum(a, b)
        d //= 2
    return seq


def _sort_desc(seq):
    if len(seq) == 1:
        return list(seq)
    half = len(seq) // 2
    return _bitonic_merge(_sort_desc(seq[:half]) + _sort_desc(seq[half:])[::-1])


def _merge_top(a, b, cap):
    n = len(a)
    if 2 * n <= cap:
        return _bitonic_merge(a + b[::-1])
    return _bitonic_merge([jnp.maximum(a[k], b[n - 1 - k]) for k in range(n)])


def _merge_sublanes(s, cap):
    d = SUBLANES // 2
    while d >= 1:
        s = _merge_top(s, [pltpu.roll(v, SUBLANES - d, 0) for v in s], cap)
        d //= 2
    return s


def _top_sorted(x, cap):
    streams = [x[SUBLANES * k:SUBLANES * (k + 1)] for k in range(x.shape[0] // SUBLANES)]
    return _merge_sublanes(_sort_desc(streams)[:cap], cap)


def _peer_kernel(x_ref, sc_ref, sh_ref, g_ref, wqh_ref, wql_ref, k3_ref, u_ref, v_ref,
                 lg_ref, lb_ref, y_ref,
                 hb_ref, q3_ref, st_ref, e_ref, tau_ref, acc_ref):
    bb, tt, d = x_ref.shape
    tn = bb * tt
    nk = st_ref.shape[1]
    eb = u_ref.shape[0] // nk
    ei = pl.program_id(2)

    @pl.when(ei == 0)
    def _():
        h2 = (x_ref[...] * (1.0 + sc_ref[...]) + sh_ref[...]).reshape(tn, d)
        hh, hl = _split(h2)
        hb_ref[...] = hh
        acc_ref[...] = jnp.zeros_like(acc_ref)

        sub = lax.broadcasted_iota(I32, (SUBLANES, tn), 0)

        def pack(rows):
            out = jnp.broadcast_to(rows[0], (SUBLANES, tn))
            for k in range(1, SUBLANES):
                out = jnp.where(sub == k, jnp.broadcast_to(rows[k], (SUBLANES, tn)), out)
            return out

        q_all = _dot3s(hh, hl, wqh_ref[...], wql_ref[...])
        dk = q_all.shape[1] // (2 * PEER_HEADS)
        for hp in range(2 * PEER_HEADS):
            qh, ql = _split(q_all[:, hp * dk:(hp + 1) * dk])
            q3_ref[hp] = jnp.concatenate([qh, ql, qh], 1)

        def per_head(h, carry):
            sv = []
            for p in range(2):
                hp = 2 * h + p
                s_t = _dot(k3_ref[hp], q3_ref[hp], NT)
                st_ref[hp] = s_t
                sv.append([v[0:1] for v in _top_sorted(s_t, PEER_TOPK)])
            sv0, sv1 = sv
            m0, m1 = sv0[0], sv1[0]
            groups = []
            for g in range(PEER_TOPK // SUBLANES):
                base = pack(sv0[SUBLANES * g:SUBLANES * (g + 1)])
                groups.append([base + sv1[b] for b in range(PEER_TOPK)])
            top = groups[0]
            for g in range(1, len(groups)):
                top = _merge_top(top, groups[g], PEER_TOPK)
            top = [v[0:1] for v in _merge_sublanes(top, PEER_TOPK)]
            mtot = m0 + m1
            z = jnp.exp(top[0] - mtot)
            for r in range(1, PEER_TOPK):
                z = z + jnp.exp(top[r] - mtot)
            tau_ref[pl.ds(h, 1), :] = top[PEER_TOPK - 1]
            inv_z = 1.0 / z
            e_ref[2 * h, :, 0:tn] = jnp.exp(st_ref[2 * h] - m0)
            e_ref[2 * h + 1, :, 0:tn] = jnp.exp(st_ref[2 * h + 1] - m1) * inv_z
            return carry

        lax.fori_loop(0, PEER_HEADS, per_head, 0)

    hb = hb_ref[...]
    act_all = _gelu(_dot(u_ref[...], hb, NT))
    acts = [act_all[e * nk:(e + 1) * nk] for e in range(eb)]
    group = 2 if eb % 2 == 0 else 1
    cw = LANE if tn % LANE == 0 else tn
    ws = [[None] * (tn // cw) for _ in range(eb)]
    for e0i in range(0, eb, group):
        tiles = [ei * eb + e0i + k for k in range(group)]
        for c in range(tn // cw):
            cs = slice(c * cw, (c + 1) * cw)
            gates = [jnp.zeros((nk, cw), F32) for _ in range(group)]
            for h in range(PEER_HEADS):
                s1 = st_ref[2 * h + 1, :, cs]
                e1 = e_ref[2 * h + 1, :, cs]
                tau = tau_ref[h:h + 1, cs]
                for k in range(group):
                    s0 = st_ref[2 * h, pl.ds(tiles[k], 1), :][:, cs]
                    e0 = e_ref[2 * h, pl.ds(tiles[k], 1), 0:tn][:, cs]
                    gates[k] = gates[k] + jnp.where(s1 + s0 >= tau, e1 * e0, 0.0)
            for k in range(group):
                ws[e0i + k][c] = (gates[k] * acts[e0i + k][:, cs]).astype(BF16)
    w_all = jnp.concatenate([jnp.concatenate(row, 1) for row in ws], 0)
    acc_ref[...] += _dot(w_all, v_ref[...], TN)

    @pl.when(ei == pl.num_programs(2) - 1)
    def _():
        y = ALPHA * x_ref[...] + g_ref[...] * acc_ref[...].reshape(bb, tt, d)
        y_ref[...] = _layer_norm(y, lg_ref[...], lb_ref[...])


def _peer(x, mod, layer, wq_hi, wq_lo, k3, u_bf, v_bf, ln_g, ln_b, bb, tt, eb):
    b, t, d = x.shape
    nt = t // tt
    tn = bb * tt
    nhp = 2 * PEER_HEADS
    dk = wq_hi.shape[-1] // nhp
    nk = k3.shape[2]
    xspec = pl.BlockSpec((bb, tt, d), lambda bi, ti, ei: (bi, ti, 0))
    vec = pl.BlockSpec((1, d), lambda bi, ti, ei: (0, 0))

    def mod_spec(j):
        return pl.BlockSpec((None, None, bb, 1, d), lambda bi, ti, ei: (layer, j, bi, 0, 0))

    wq_spec = pl.BlockSpec((None, d, nhp * dk), lambda bi, ti, ei: (layer, 0, 0))
    k_spec = pl.BlockSpec((None, nhp, nk, 3 * dk), lambda bi, ti, ei: (layer, 0, 0, 0))
    tab_spec = pl.BlockSpec((None, eb * nk, d), lambda bi, ti, ei: (layer, ei, 0))
    return pl.pallas_call(
        _peer_kernel,
        grid=(b // bb, nt, nk // eb),
        in_specs=[xspec, mod_spec(4), mod_spec(3), mod_spec(5), wq_spec, wq_spec, k_spec,
                  tab_spec, tab_spec, vec, vec],
        out_specs=xspec,
        out_shape=jax.ShapeDtypeStruct((b, t, d), F32),
        scratch_shapes=[pltpu.VMEM((tn, d), BF16), pltpu.VMEM((nhp, tn, 3 * dk), BF16),
                        pltpu.VMEM((nhp, nk, tn), F32), pltpu.VMEM((nhp, nk, tn + LANE), F32),
                        pltpu.VMEM((PEER_HEADS, tn), F32),
                        pltpu.VMEM((tn, d), F32)],
        compiler_params=_cparams("arbitrary", "arbitrary", "arbitrary"),
        name="peer",
    )(x, mod, mod, mod, wq_hi, wq_lo, k3, u_bf, v_bf, ln_g, ln_b)


def _block_diag(w):
    l, nb, bw, _ = w.shape
    eye = jnp.eye(nb, dtype=w.dtype)
    return jnp.einsum("lncd,nm->lncmd", w, eye).reshape(l, nb * bw, nb * bw)


def _pick_tile(t, target):
    tt = min(t, target)
    while t % tt:
        tt //= 2
    return tt


def kernel(x_prompt, x_sample, cache_k, cache_v, cache_idx_k, state_conv, state_lru, page_table,
           c_prompt, c_sample, w_ada, b_ada, w_in, conv_w, conv_b, lru_wa, lru_ba, lru_wx, lru_bx,
           lru_lambda, beta_lru, beta_attn, w_out, ln1_g, ln1_b, ln2_g, ln2_b, peer_wq, peer_keys,
           peer_u, peer_v):
    depth = w_ada.shape[0]
    bp, tp, d = x_prompt.shape
    bs, ts, _ = x_sample.shape
    lw = d // 2
    aw = N_HEADS * HEAD_DIM
    page = cache_k.shape[2]
    past = page_table.shape[1] * page
    assert tp >= CONV_W - 1 and ts >= CONV_W - 1 and ts % 8 == 0 and bs % 8 == 0
    assert cache_k.shape[3:] == (N_HEADS, HEAD_DIM) and page >= ts and HEAD_DIM == 64

    in_cols = w_in.shape[-1]
    cols_pad = -(-in_cols // LANE) * LANE
    w_in_hi, w_in_lo = _split(jnp.pad(w_in, ((0, 0), (0, 0), (0, cols_pad - in_cols))))
    w_out_hi, w_out_lo = _split(w_out)
    wa_hi, wa_lo = _split(_block_diag(lru_wa))
    wx_hi, wx_lo = _split(_block_diag(lru_wx))
    nhp = 2 * PEER_HEADS
    dk = peer_wq.shape[-1] // nhp
    wq_hi, wq_lo = _split(peer_wq)
    nk = peer_keys.shape[3]
    pk = peer_keys.transpose(0, 2, 1, 3, 4).reshape(depth, nhp, nk, dk)
    pk_hi, pk_lo = _split(pk)
    pk3 = jnp.concatenate([pk_hi, pk_hi, pk_lo], -1)
    u_bf = peer_u.astype(BF16)
    v_bf = peer_v.astype(BF16)
    eb = 4 if nk % 4 == 0 else 1
    pool_k = cache_k.transpose(0, 1, 3, 4, 2)
    pool_v = cache_v.transpose(0, 1, 3, 4, 2)
    pool_ki = cache_idx_k.transpose(0, 1, 3, 2)

    nb_all = bp + bs
    nb_pad = -(-nb_all // 8) * 8
    c_all = jnp.pad(jnp.concatenate([c_prompt, c_sample], 0), ((0, nb_pad - nb_all), (0, 0)))
    mod = _ada_mod(c_all, w_ada, b_ada)
    mod_p = mod[:, :, :bp].reshape(depth, 6, bp, 1, d)
    mod_s = mod[:, :, bp:nb_all].reshape(depth, 6, bs, 1, d)

    tt_p = _pick_tile(tp, 512)
    tabs_p = [a.reshape(tp // tt_p, tt_p, LANE) for a in _rope_tables(jnp.arange(tp, dtype=I32))]
    bb_s = _pick_tile(bs, 32)
    tabs_s = [jnp.tile(a, (bb_s, 1)).reshape(1, bb_s * ts, LANE)
              for a in _rope_tables(past + jnp.arange(ts, dtype=I32))]

    tt_lru = _pick_tile(tp, 256)
    tq = _pick_tile(tp, Q_BLOCK)
    n_ranges = max(r for r in (1, 2, 4, 8) if (tp // tq) % r == 0)
    tn_p = _pick_tile(tp, 512)
    bb_peer = _pick_tile(bs, max(1, 512 // ts))

    yp, ys = x_prompt, x_sample
    outs = {k: [] for k in ("kp", "vp", "kip", "cp", "hp", "ks", "vs", "kis", "cs", "hs")}
    for l in range(depth):
        lru_w = (conv_w[l], conv_b[l][None], wa_hi[l], wa_lo[l], lru_ba[l][None], wx_hi[l], wx_lo[l],
                 lru_bx[l][None], lru_lambda[l][None], beta_lru[l][None])
        zx, zg, q, k, v, qi, tail, ki, ki3, k16, v16 = _in_proj(yp, mod_p, l, w_in_hi, w_in_lo,
                                                                tabs_p, 1, tt_p)
        zc = jnp.zeros((bp, CONV_W - 1, lw), F32)
        zh = jnp.zeros((bp, lw), F32)
        mix_lru, nconv, nh = _lru_prompt(zx, zg, zc, zh, lru_w, bp, tp, tt_lru)
        o = _attn_prompt(qi, tail, ki3, q, k16, v16, bp, tp, tq, n_ranges)
        x1 = _out_proj(yp, mod_p, l, mix_lru, o, beta_attn[l][None], w_out_hi, w_out_lo,
                       ln1_g[l][None], ln1_b[l][None], 1, tt_p)
        yp = _peer(x1, mod_p, l, wq_hi, wq_lo, pk3, u_bf, v_bf,
                   ln2_g[l][None], ln2_b[l][None], 1, tn_p, eb)
        outs["kp"].append(k.reshape(bp, tp, N_HEADS, HEAD_DIM))
        outs["vp"].append(v.reshape(bp, tp, N_HEADS, HEAD_DIM))
        outs["kip"].append(ki.reshape(bp, tp, IDX_DIM))
        outs["cp"].append(nconv)
        outs["hp"].append(nh.reshape(bp, lw))
        zx, zg, q, k, v, qi, tail, ki = _in_proj(ys, mod_s, l, w_in_hi, w_in_lo, tabs_s, bb_s, ts)[:8]
        tm = lambda a: a.reshape(bs, ts, lw).transpose(1, 0, 2)
        mix_t, nconv_t, nh = _lru_sample(tm(zx), tm(zg), state_conv[l].transpose(1, 0, 2),
                                         state_lru[l], lru_w)
        mix_lru = mix_t.transpose(1, 0, 2).reshape(bs * ts, lw)
        o = _attn_sample(page_table, qi, tail, q, k, v, pool_ki, pool_k, pool_v, l, bs, ts)
        x1 = _out_proj(ys, mod_s, l, mix_lru, o, beta_attn[l][None], w_out_hi, w_out_lo,
                       ln1_g[l][None], ln1_b[l][None], bb_s, ts)
        ys = _peer(x1, mod_s, l, wq_hi, wq_lo, pk3, u_bf, v_bf,
                   ln2_g[l][None], ln2_b[l][None], bb_peer, ts, eb)
        outs["ks"].append(k.reshape(bs, ts, N_HEADS, HEAD_DIM))
        outs["vs"].append(v.reshape(bs, ts, N_HEADS, HEAD_DIM))
        outs["kis"].append(ki.reshape(bs, ts, IDX_DIM))
        outs["cs"].append(nconv_t.transpose(1, 0, 2))
        outs["hs"].append(nh)

    st = lambda name: jnp.stack(outs[name])
    return (yp, ys, st("kp"), st("vp"), st("kip"), st("cp"), st("hp"),
            st("ks"), st("vs"), st("kis"), st("cs"), st("hs"))
```

```python
import functools
import math

import jax
import jax.numpy as jnp
from jax import lax
from jax.experimental import pallas as pl
from jax.experimental.pallas import tpu as pltpu

F32 = jnp.float32
BF16 = jnp.bfloat16
I32 = jnp.int32

LRU_C = 8.0
CONV_W = 4
N_HEADS = 8
HEAD_DIM = 64
IDX_HEADS = 4
IDX_DIM = 64
TOPK_MAX = 256
ROPE_THETA = 500000.0
ROPE_FRAC_DIV = 4
Q_BLOCK = 128
PEER_HEADS = 8
PEER_TOPK = 16
DEPTH_NOMINAL = 4
ALPHA = (2 * DEPTH_NOMINAL) ** 0.25
LN_EPS = 1e-5

LANE = 128
VMEM_LIMIT = 56 * 1024 * 1024

NN = (((1,), (0,)), ((), ()))
NT = (((1,), (1,)), ((), ()))
TN = (((0,), (0,)), ((), ()))

NEG_INF = float("-inf")
INT_MIN = -2147483648
KEY_NEG_INF = -2139095041


def _cparams(*sem):
    return pltpu.CompilerParams(dimension_semantics=sem, vmem_limit_bytes=VMEM_LIMIT)


def _split(a):
    hi = a.astype(BF16)
    lo = (a - hi.astype(F32)).astype(BF16)
    return hi, lo


def _dot(a, b, dims=NN):
    return lax.dot_general(a, b, dims, preferred_element_type=F32)


def _dot3s(ah, al, bh, bl, dims=NN):
    return _dot(ah, bh, dims) + (_dot(ah, bl, dims) + _dot(al, bh, dims))


def _dot3w(a, bh, bl, dims=NN):
    ah, al = _split(a)
    return _dot3s(ah, al, bh, bl, dims)


def _cat3(x, lhs):
    hi = x.astype(BF16).astype(F32)
    lo = (x - hi).astype(BF16).astype(F32)
    parts = [hi, hi, lo] if lhs else [hi, lo, hi]
    return jnp.concatenate(parts, 1).astype(BF16)


def _gelu(x):
    return 0.5 * x * (1.0 + lax.erf(x * (1.0 / math.sqrt(2.0))))


def _layer_norm(y, g, b):
    mu = jnp.mean(y, -1, keepdims=True)
    d = y - mu
    var = jnp.mean(d * d, -1, keepdims=True)
    return d * lax.rsqrt(var + LN_EPS) * g + b


def _rms(y, g):
    return y * lax.rsqrt(jnp.mean(y * y, -1, keepdims=True) + LN_EPS) * g


def _order_key(x):
    b = pltpu.bitcast(x, I32)
    return jnp.where(b < 0, b ^ 0x7FFFFFFF, b)


def _ada_kernel(c_ref, w_ref, b_ref, o_ref):
    c = c_ref[...]
    s = c * jax.nn.sigmoid(c)
    wh, wl = _split(w_ref[...])
    o_ref[...] = _dot3w(s, wh, wl) + b_ref[...]


def _ada_mod(c_all, w_ada, b_ada):
    depth, d, six_d = w_ada.shape
    nb = c_all.shape[0]
    return pl.pallas_call(
        _ada_kernel,
        grid=(depth, six_d // d),
        in_specs=[
            pl.BlockSpec((nb, d), lambda l, j: (0, 0)),
            pl.BlockSpec((None, d, d), lambda l, j: (l, 0, j)),
            pl.BlockSpec((None, None, 1, d), lambda l, j: (l, j, 0, 0)),
        ],
        out_specs=pl.BlockSpec((None, None, nb, d), lambda l, j: (l, j, 0, 0)),
        out_shape=jax.ShapeDtypeStruct((depth, six_d // d, nb, d), F32),
        compiler_params=_cparams("arbitrary", "arbitrary"),
        name="ada_mod",
    )(c_all, w_ada, b_ada.reshape(depth, six_d // d, 1, d))


def _rope(x, c, s1, s2):
    w = x.shape[-1]
    return x * c + pltpu.roll(x, 8, 1) * s1 + pltpu.roll(x, w - 8, 1) * s2


def _inproj_kernel(x_ref, sc_ref, sh_ref, wh_ref, wl_ref, rc_ref, rs1_ref, rs2_ref,
                   tc_ref, ts1_ref, ts2_ref,
                   zx_ref, zg_ref, q_ref, k_ref, v_ref, qi_ref, tail_ref, ki_ref, ki3_ref,
                   kb_ref, vb_ref):
    bb, tt, d = x_ref.shape
    h = x_ref[...] * (1.0 + sc_ref[...]) + sh_ref[...]
    hh, hl = _split(h.reshape(bb * tt, d))

    def proj(lo, hi):
        return _dot3s(hh, hl, wh_ref[:, lo:hi], wl_ref[:, lo:hi])

    def proj1(lo, hi):
        return _dot(hh, wh_ref[:, lo:hi])

    lw = zx_ref.shape[-1]
    aw = q_ref.shape[-1]
    iw = qi_ref.shape[-1]
    o = 0
    zx_ref[...] = proj1(o, o + lw); o += lw
    zg_ref[...] = proj1(o, o + lw); o += lw
    rc, rs1, rs2 = rc_ref[...], rs1_ref[...], rs2_ref[...]
    rep = aw // LANE
    c4 = jnp.concatenate([rc] * rep, 1)
    s14 = jnp.concatenate([rs1] * rep, 1)
    s24 = jnp.concatenate([rs2] * rep, 1)
    q_ref[...] = _rope(proj1(o, o + aw), c4, s14, s24); o += aw
    kk = _rope(proj1(o, o + aw), c4, s14, s24); o += aw
    k_ref[...] = kk
    kb_ref[...] = kk.astype(BF16)
    vv = proj1(o, o + aw); o += aw
    v_ref[...] = vv
    vb_ref[...] = vv.astype(BF16)
    repi = iw // LANE
    qi_ref[...] = _rope(proj(o, o + iw), jnp.concatenate([rc] * repi, 1),
                        jnp.concatenate([rs1] * repi, 1), jnp.concatenate([rs2] * repi, 1)); o += iw
    tail = _rope(proj(o, o + LANE), tc_ref[...], ts1_ref[...], ts2_ref[...])
    tail_ref[...] = tail
    ki_ref[...] = tail[:, :IDX_DIM]
    ki3_ref[...] = _cat3(tail[:, :IDX_DIM], lhs=False)


def _rope_tables(pos):
    t = pos.shape[0]
    rot = HEAD_DIM // ROPE_FRAC_DIV
    half = rot // 2
    freqs = ROPE_THETA ** (-jnp.arange(half, dtype=F32) / half)
    ang = pos.astype(F32)[:, None] * freqs[None, :]
    cos, sin = jnp.cos(ang), jnp.sin(ang)
    ones = jnp.ones((t, HEAD_DIM - rot), F32)
    zeros = jnp.zeros((t, HEAD_DIM - rot), F32)
    zh = jnp.zeros((t, half), F32)
    c64 = jnp.concatenate([cos, cos, ones], 1)
    s1_64 = jnp.concatenate([zh, sin, zeros], 1)
    s2_64 = jnp.concatenate([-sin, zh, zeros], 1)
    rc = jnp.concatenate([c64, c64], 1)
    rs1 = jnp.concatenate([s1_64, s1_64], 1)
    rs2 = jnp.concatenate([s2_64, s2_64], 1)
    wscale = jnp.concatenate([jnp.full((t, IDX_HEADS), IDX_HEADS ** -0.5, F32),
                              jnp.ones((t, LANE - IDX_DIM - IDX_HEADS), F32)], 1)
    z64 = jnp.zeros((t, LANE - IDX_DIM), F32)
    tc = jnp.concatenate([c64, wscale], 1)
    ts1 = jnp.concatenate([s1_64, z64], 1)
    ts2 = jnp.concatenate([s2_64, z64], 1)
    return rc, rs1, rs2, tc, ts1, ts2


def _in_proj(x, mod, layer, w_hi, w_lo, tables, bb, tt):
    b, t, d = x.shape
    n = b * t
    rows = bb * tt
    nt = t // tt
    lw = d // 2
    aw = N_HEADS * HEAD_DIM
    iw = IDX_HEADS * IDX_DIM
    cols = w_hi.shape[-1]
    tab_idx = (lambda bi, ti: (ti, 0, 0)) if tables[0].shape[0] > 1 else (lambda bi, ti: (0, 0, 0))
    tab_spec = pl.BlockSpec((None, rows, LANE), tab_idx)
    row_idx = lambda bi, ti: (bi * nt + ti, 0)

    def mod_spec(j):
        return pl.BlockSpec((None, None, bb, 1, d), lambda bi, ti: (layer, j, bi, 0, 0))

    outs = [(n, lw), (n, lw), (n, aw), (n, aw), (n, aw), (n, iw), (n, LANE), (n, IDX_DIM),
            (n, 3 * IDX_DIM), (n, aw), (n, aw)]
    dtypes = [F32] * 8 + [BF16] * 3
    return pl.pallas_call(
        _inproj_kernel,
        grid=(b // bb, nt),
        in_specs=[
            pl.BlockSpec((bb, tt, d), lambda bi, ti: (bi, ti, 0)),
            mod_spec(1), mod_spec(0),
            pl.BlockSpec((None, d, cols), lambda bi, ti: (layer, 0, 0)),
            pl.BlockSpec((None, d, cols), lambda bi, ti: (layer, 0, 0)),
        ] + [tab_spec] * 6,
        out_specs=[pl.BlockSpec((rows, w), row_idx) for _, w in outs],
        out_shape=[jax.ShapeDtypeStruct(s, dt) for s, dt in zip(outs, dtypes)],
        compiler_params=_cparams("arbitrary", "arbitrary"),
        name="in_proj",
    )(x, mod, mod, w_hi, w_lo, *tables)


def _lru_gates(xc, wah, wal, ba, wxh, wxl, bx, lam):
    xh, xl = _split(xc)
    r = jax.nn.sigmoid(_dot3s(xh, xl, wah, wal) + ba)
    i = jax.nn.sigmoid(_dot3s(xh, xl, wxh, wxl) + bx)
    nl = -lam
    sp = jnp.maximum(nl, 0.0) + jnp.log1p(jnp.exp(-jnp.abs(nl)))
    log_a = -LRU_C * r * sp
    a = jnp.exp(log_a)
    u = jnp.sqrt(1.0 - a * a) * (i * xc)
    return a, u


def _lru_prompt_kernel(zx_ref, zg_ref, cbuf_ref, h0_ref, cw_ref, cb_ref, wah_ref, wal_ref, ba_ref,
                       wxh_ref, wxl_ref, bx_ref, lam_ref, beta_ref,
                       mix_ref, nconv_ref, nh_ref, ext_ref, hst_ref):
    tt = zx_ref.shape[0]
    ti = pl.program_id(1)

    @pl.when(ti == 0)
    def _():
        ext_ref[5:8, :] = cbuf_ref[...]
        hst_ref[0:1, :] = h0_ref[...]

    x = zx_ref[...]
    ext_ref[8:8 + tt, :] = x
    w = cw_ref[...]
    xc = (cb_ref[...] + ext_ref[5:5 + tt, :] * w[0:1] + ext_ref[6:6 + tt, :] * w[1:2]
          + ext_ref[7:7 + tt, :] * w[2:3] + x * w[3:4])
    tail3 = ext_ref[tt + 5:tt + 8, :]
    ext_ref[5:8, :] = tail3

    a, u = _lru_gates(xc, wah_ref[...], wal_ref[...], ba_ref[...], wxh_ref[...], wxl_ref[...],
                      bx_ref[...], lam_ref[...])
    row = lax.broadcasted_iota(I32, (tt, 1), 0)
    s = 1
    while s < tt:
        a_sh = pltpu.roll(a, s, 0)
        u_sh = pltpu.roll(u, s, 0)
        m = row >= s
        u = u + a * jnp.where(m, u_sh, 0.0)
        a = a * jnp.where(m, a_sh, 1.0)
        s *= 2
    h = a * hst_ref[0:1, :] + u
    hst_ref[0:1, :] = h[tt - 1:tt, :]
    y = _gelu(zg_ref[...]) * h
    mix_ref[...] = _rms(y, beta_ref[...])

    @pl.when(ti == pl.num_programs(1) - 1)
    def _():
        nconv_ref[...] = tail3
        nh_ref[...] = h[tt - 1:tt, :]


def _lru_prompt(zx, zg, cbuf, h0, lw, b, t, tt):
    n, w = zx.shape
    nt = t // tt
    row_spec = pl.BlockSpec((tt, w), lambda bi, ti: (bi * nt + ti, 0))
    full = lambda shape: pl.BlockSpec(shape, lambda bi, ti: tuple(0 for _ in shape))
    vec = full((1, w))
    mat = full((w, w))
    return pl.pallas_call(
        _lru_prompt_kernel,
        grid=(b, nt),
        in_specs=[row_spec, row_spec,
                  pl.BlockSpec((None, CONV_W - 1, w), lambda bi, ti: (bi, 0, 0)),
                  pl.BlockSpec((None, 1, w), lambda bi, ti: (bi, 0, 0)),
                  full((CONV_W, w)), vec, mat, mat, vec, mat, mat, vec, vec, vec],
        out_specs=[row_spec,
                   pl.BlockSpec((None, CONV_W - 1, w), lambda bi, ti: (bi, 0, 0)),
                   pl.BlockSpec((None, 1, w), lambda bi, ti: (bi, 0, 0))],
        out_shape=[jax.ShapeDtypeStruct((n, w), F32),
                   jax.ShapeDtypeStruct((b, CONV_W - 1, w), F32),
                   jax.ShapeDtypeStruct((b, 1, w), F32)],
        scratch_shapes=[pltpu.VMEM((tt + 8, w), F32), pltpu.VMEM((8, w), F32)],
        compiler_params=_cparams("arbitrary", "arbitrary"),
        name="lru_prompt",
    )(zx, zg, cbuf, h0.reshape(b, 1, w), *lw)


def _lru_sample_kernel(zx_ref, zg_ref, cbuf_ref, h0_ref, cw_ref, cb_ref, wah_ref, wal_ref, ba_ref,
                       wxh_ref, wxl_ref, bx_ref, lam_ref, beta_ref,
                       mix_ref, nconv_ref, nh_ref, xc_ref):
    t, b, wd = zx_ref.shape
    w = cw_ref[...]
    xp = [cbuf_ref[k] for k in range(CONV_W - 1)] + [zx_ref[k] for k in range(t)]
    for k in range(t):
        xc_ref[k * b:(k + 1) * b, :] = (cb_ref[...] + xp[k] * w[0:1] + xp[k + 1] * w[1:2]
                                        + xp[k + 2] * w[2:3] + xp[k + 3] * w[3:4])
    for k in range(CONV_W - 1):
        nconv_ref[k] = xp[t + k]
    a, u = _lru_gates(xc_ref[...], wah_ref[...], wal_ref[...], ba_ref[...], wxh_ref[...],
                      wxl_ref[...], bx_ref[...], lam_ref[...])
    h = h0_ref[...]
    beta = beta_ref[...]
    for k in range(t):
        h = a[k * b:(k + 1) * b, :] * h + u[k * b:(k + 1) * b, :]
        mix_ref[k] = _rms(_gelu(zg_ref[k]) * h, beta)
    nh_ref[...] = h


def _lru_sample(zx_t, zg_t, cbuf_t, h0, lw):
    t, b, w = zx_t.shape
    full = lambda shape: pl.BlockSpec(shape, lambda i: tuple(0 for _ in shape))
    vec = full((1, w))
    mat = full((w, w))
    return pl.pallas_call(
        _lru_sample_kernel,
        grid=(1,),
        in_specs=[full((t, b, w)), full((t, b, w)), full((CONV_W - 1, b, w)), full((b, w)),
                  full((CONV_W, w)), vec, mat, mat, vec, mat, mat, vec, vec, vec],
        out_specs=[full((t, b, w)), full((CONV_W - 1, b, w)), full((b, w))],
        out_shape=[jax.ShapeDtypeStruct((t, b, w), F32),
                   jax.ShapeDtypeStruct((CONV_W - 1, b, w), F32),
                   jax.ShapeDtypeStruct((b, w), F32)],
        scratch_shapes=[pltpu.VMEM((t * b, w), F32)],
        compiler_params=_cparams("arbitrary"),
        name="lru_sample",
    )(zx_t, zg_t, cbuf_t, h0, *lw)


def _select_bias(score, qpos, key_ref, topk):
    q, s = score.shape
    kpos = lax.broadcasted_iota(I32, (q, s), 1)
    vis = kpos <= qpos
    key_ref[...] = jnp.where(vis, _order_key(score + 0.0), KEY_NEG_INF)

    kf = float(topk)

    def count_ge(t):
        return jnp.sum(jnp.where(key_ref[...] >= t, 1.0, 0.0), axis=1, keepdims=True)

    c0 = count_ge(jnp.zeros((q, 1), I32))
    small = (qpos + 1) <= topk
    t0 = jnp.where(small, KEY_NEG_INF + 1, jnp.where(c0 >= kf, 0, INT_MIN)).astype(I32)
    r0 = jnp.where(small, 1, jnp.where(c0 == kf, 1, 0)).astype(I32)

    def vcond(c):
        i, _, res = c
        return jnp.logical_and(i < 31, jnp.min(res) == 0)

    def vbody(c):
        i, t, res = c
        cand = t | jnp.left_shift(jnp.int32(1), 30 - i)
        cnt = count_ge(cand)
        t = jnp.where(res > 0, t, jnp.where(cnt >= kf, cand, t))
        res = jnp.where(cnt == kf, 1, res)
        return i + 1, t, res

    _, thr, res = lax.while_loop(vcond, vbody, (jnp.int32(0), t0, r0))
    nbits = max(1, (s - 1).bit_length())

    def tie_phase():
        c_gt = jnp.sum(jnp.where(key_ref[...] > thr, 1.0, 0.0), axis=1, keepdims=True)
        need = kf - c_gt

        def ibody(i, j):
            cand = j + jnp.left_shift(jnp.int32(1), nbits - 1 - i)
            kp = lax.broadcasted_iota(I32, (q, s), 1)
            hit = jnp.where(key_ref[...] == thr, jnp.where(kp <= cand, 1.0, 0.0), 0.0)
            f = jnp.sum(hit, axis=1, keepdims=True)
            return jnp.where(f < need, cand, j)

        return lax.fori_loop(0, nbits, ibody, jnp.full((q, 1), -1, I32)) + 1

    jsel = lax.cond(jnp.min(res) == 0, tie_phase, lambda: jnp.full((q, 1), s, I32))
    jsel = jnp.where(res > 0, s, jsel)
    key = key_ref[...]
    sel = jnp.where(key > thr, 1, jnp.where(key == thr, jnp.where(kpos <= jsel, 1, 0), 0))
    sel = jnp.where(vis, sel, 0)
    return jnp.where(sel > 0, 0.0, NEG_INF)


def _attn_prompt_kernel(qi_ref, tailq_ref, ki3_ref, q_ref, k_ref, v_ref, o_ref, bias_ref, key_ref,
                        *, topk, qb0):
    tq = q_ref.shape[0]
    ki3 = ki3_ref[...]
    tqv = tailq_ref[...]
    qi = qi_ref[...]
    score = None
    for h in range(IDX_HEADS):
        lg = _dot(_cat3(qi[:, h * IDX_DIM:(h + 1) * IDX_DIM], lhs=True), ki3, NT)
        term = tqv[:, IDX_DIM + h:IDX_DIM + h + 1] * jnp.maximum(lg, 0.0)
        score = term if score is None else score + term
    qpos = (qb0 + pl.program_id(1)) * tq + lax.broadcasted_iota(I32, (tq, 1), 0)
    bias_ref[0:tq, :] = _select_bias(score, qpos, key_ref, topk)
    bias_ref[tq:2 * tq, :] = bias_ref[0:tq, :]

    low = lax.broadcasted_iota(I32, (tq, LANE), 1) < HEAD_DIM
    for g in range(q_ref.shape[1] // LANE):
        cs = slice(g * LANE, (g + 1) * LANE)
        qg = q_ref[:, cs] * (HEAD_DIM ** -0.5)
        qpair = jnp.concatenate([jnp.where(low, qg, 0.0), jnp.where(low, 0.0, qg)], 0).astype(BF16)
        s = _dot(qpair, k_ref[:, cs], NT) + bias_ref[...]
        m = jnp.max(s, axis=1, keepdims=True)
        p = jnp.exp(s - m)
        l = jnp.sum(p, axis=1, keepdims=True)
        og = _dot(p.astype(BF16), v_ref[:, cs]) / l
        o_ref[:, cs] = jnp.where(low, og[0:tq], og[tq:2 * tq])


def _attn_prompt(qi, tail, ki3, q, k_bf, v_bf, b, t, tq, n_ranges):
    nq = t // tq
    topk = min(TOPK_MAX, t // 4)
    iw = qi.shape[-1]
    aw = q.shape[-1]
    nqr = nq // n_ranges
    keys3 = lambda a: a.reshape(b, t, a.shape[-1])
    outs = []
    for r in range(n_ranges):
        qb0 = r * nqr
        tv = (r + 1) * nqr * tq
        qrow = lambda w, qb0=qb0: pl.BlockSpec((tq, w), lambda bi, qb: (bi * nq + qb0 + qb, 0))
        kvis = lambda w, tv=tv: pl.BlockSpec((None, tv, w), lambda bi, qb: (bi, 0, 0))
        outs.append(pl.pallas_call(
            functools.partial(_attn_prompt_kernel, topk=topk, qb0=qb0),
            grid=(b, nqr),
            in_specs=[qrow(iw), qrow(LANE), kvis(3 * IDX_DIM), qrow(aw), kvis(aw), kvis(aw)],
            out_specs=pl.BlockSpec((None, tq, aw), lambda bi, qb: (bi, qb, 0)),
            out_shape=jax.ShapeDtypeStruct((b, nqr * tq, aw), F32),
            scratch_shapes=[pltpu.VMEM((2 * tq, tv), F32), pltpu.VMEM((tq, tv), I32)],
            compiler_params=_cparams("arbitrary", "arbitrary"),
            name="attn_prompt",
        )(qi, tail, keys3(ki3), q, keys3(k_bf), keys3(v_bf)))
    return jnp.concatenate(outs, 1).reshape(b * t, aw)


def _attn_sample_kernel(pt_ref, qi_ref, tail_ref, q_ref, kn_ref, vn_ref, *rest, n_pages, page, topk):
    ki_pages = rest[:n_pages]
    k_pages = rest[n_pages:2 * n_pages]
    v_pages = rest[2 * n_pages:3 * n_pages]
    o_ref, kit_ref, kst_ref, vst_ref, key_ref = rest[3 * n_pages:]
    t = q_ref.shape[0]
    past = n_pages * page
    hsl = lambda h: slice(h * HEAD_DIM, (h + 1) * HEAD_DIM)
    for j in range(n_pages):
        cols = slice(j * page, (j + 1) * page)
        kit = ki_pages[j][...]
        hi = kit.astype(BF16)
        lo = (kit - hi.astype(F32)).astype(BF16)
        kit_ref[0:IDX_DIM, cols] = hi
        kit_ref[IDX_DIM:2 * IDX_DIM, cols] = lo
        kit_ref[2 * IDX_DIM:3 * IDX_DIM, cols] = hi
        kst_ref[:, :, cols] = k_pages[j][...].astype(BF16)
        vst_ref[:, :, cols] = v_pages[j][...].astype(BF16)
    tl = tail_ref[...]
    ki_new = _cat3(jnp.concatenate([tl[:, :IDX_DIM], jnp.zeros((page - t, IDX_DIM), F32)], 0),
                   lhs=False)
    kn = kn_ref[...]
    vn = vn_ref[...]
    zpad = jnp.zeros((page - t, HEAD_DIM), F32)

    qi = qi_ref[...]
    qi_rows = jnp.concatenate([qi[:, h * IDX_DIM:(h + 1) * IDX_DIM] for h in range(IDX_HEADS)], 0)
    w_rows = jnp.concatenate([tl[:, IDX_DIM + h:IDX_DIM + h + 1] for h in range(IDX_HEADS)], 0)
    qi3 = _cat3(qi_rows, lhs=True)
    logits = jnp.concatenate([_dot(qi3, kit_ref[...]), _dot(qi3, ki_new, NT)], 1)
    sc = jnp.maximum(logits, 0.0) * w_rows
    score = sc[0:t]
    for h in range(1, IDX_HEADS):
        score = score + sc[h * t:(h + 1) * t]
    qpos = past + lax.broadcasted_iota(I32, (t, 1), 0)
    bias = _select_bias(score, qpos, key_ref, topk)

    q = q_ref[...] * (HEAD_DIM ** -0.5)
    rpad = -t % 16
    outs = []
    for h in range(N_HEADS):
        qh = jnp.concatenate([q[:, hsl(h)], jnp.zeros((rpad, HEAD_DIM), F32)], 0).astype(BF16)
        k_new = jnp.concatenate([kn[:, hsl(h)], zpad], 0).astype(BF16)
        v_new = jnp.concatenate([vn[:, hsl(h)], zpad], 0).astype(BF16)
        s = jnp.concatenate([_dot(qh, kst_ref[h]), _dot(qh, k_new, NT)], 1)[0:t] + bias
        m = jnp.max(s, axis=1, keepdims=True)
        p = jnp.exp(s - m)
        l = jnp.sum(p, axis=1, keepdims=True)
        ph = jnp.concatenate([p, jnp.zeros((rpad, past + page), F32)], 0).astype(BF16)
        o = _dot(ph[:, :past], vst_ref[h], NT) + _dot(ph[:, past:], v_new)
        outs.append(o[0:t] / l)
    o_ref[...] = jnp.concatenate(outs, 1)


def _attn_sample(page_table, qi, tail, q, kn, vn, pool_ki, pool_k, pool_v, layer, b, t):
    n_pages = page_table.shape[1]
    page = pool_ki.shape[3]
    past = n_pages * page
    topk = min(TOPK_MAX, (past + t) // 4)
    aw = q.shape[-1]
    iw = qi.shape[-1]
    s_pad = past + page
    row = lambda w: pl.BlockSpec((t, w), lambda bi, pt: (bi, 0))

    def ki_spec(j):
        return pl.BlockSpec((None, None, IDX_DIM, page), lambda bi, pt: (layer, pt[bi, j], 0, 0))

    def kv_spec(j):
        return pl.BlockSpec((None, None, N_HEADS, HEAD_DIM, page),
                            lambda bi, pt: (layer, pt[bi, j], 0, 0, 0))

    in_specs = [row(iw), row(LANE), row(aw), row(aw), row(aw)]
    in_specs += [ki_spec(j) for j in range(n_pages)]
    in_specs += [kv_spec(j) for j in range(n_pages)]
    in_specs += [kv_spec(j) for j in range(n_pages)]
    grid_spec = pltpu.PrefetchScalarGridSpec(
        num_scalar_prefetch=1,
        grid=(b,),
        in_specs=in_specs,
        out_specs=row(aw),
        scratch_shapes=[pltpu.VMEM((3 * IDX_DIM, past), BF16),
                        pltpu.VMEM((N_HEADS, HEAD_DIM, past), BF16),
                        pltpu.VMEM((N_HEADS, HEAD_DIM, past), BF16),
                        pltpu.VMEM((t, s_pad), I32)],
    )
    return pl.pallas_call(
        functools.partial(_attn_sample_kernel, n_pages=n_pages, page=page, topk=topk),
        grid_spec=grid_spec,
        out_shape=jax.ShapeDtypeStruct((b * t, aw), F32),
        compiler_params=_cparams("arbitrary"),
        name="attn_sample",
    )(page_table, qi, tail, q, kn, vn, *([pool_ki] * n_pages), *([pool_k] * n_pages),
      *([pool_v] * n_pages))


def _outproj_kernel(x_ref, g_ref, ml_ref, o_ref, beta_ref, wh_ref, wl_ref, lg_ref, lb_ref, y_ref):
    bb, tt, d = x_ref.shape
    lw = ml_ref.shape[-1]
    mo = _rms(o_ref[...], beta_ref[...])
    acc = (_dot3w(ml_ref[...], wh_ref[0:lw, :], wl_ref[0:lw, :])
           + _dot3w(mo, wh_ref[lw:, :], wl_ref[lw:, :]))
    y = ALPHA * x_ref[...] + g_ref[...] * acc.reshape(bb, tt, d)
    y_ref[...] = _layer_norm(y, lg_ref[...], lb_ref[...])


def _out_proj(x, mod, layer, mix_lru, o, beta_attn, w_hi, w_lo, ln_g, ln_b, bb, tt):
    b, t, d = x.shape
    nt = t // tt
    rows = bb * tt
    lw = mix_lru.shape[-1]
    aw = o.shape[-1]
    xspec = pl.BlockSpec((bb, tt, d), lambda bi, ti: (bi, ti, 0))
    vec = lambda w: pl.BlockSpec((1, w), lambda bi, ti: (0, 0))
    wspec = pl.BlockSpec((None, lw + aw, d), lambda bi, ti: (layer, 0, 0))
    return pl.pallas_call(
        _outproj_kernel,
        grid=(b // bb, nt),
        in_specs=[xspec,
                  pl.BlockSpec((None, None, bb, 1, d), lambda bi, ti: (layer, 2, bi, 0, 0)),
                  pl.BlockSpec((rows, lw), lambda bi, ti: (bi * nt + ti, 0)),
                  pl.BlockSpec((rows, aw), lambda bi, ti: (bi * nt + ti, 0)),
                  vec(aw), wspec, wspec, vec(d), vec(d)],
        out_specs=xspec,
        out_shape=jax.ShapeDtypeStruct((b, t, d), F32),
        compiler_params=_cparams("arbitrary", "arbitrary"),
        name="out_proj",
    )(x, mod, mix_lru, o, beta_attn, w_hi, w_lo, ln_g, ln_b)


SUBLANES = 8


def _bitonic_merge(seq):
    seq = list(seq)
    n = len(seq)
    d = n // 2
    while d >= 1:
        for i in range(n):
            if i & d == 0:
                a, b = seq[i], seq[i + d]
                seq[i], seq[i + d] = jnp.maximum(a, b), jnp.minimum(a, b)
        d //= 2
    return seq


def _sort_desc(seq):
    if len(seq) == 1:
        return list(seq)
    half = len(seq) // 2
    return _bitonic_merge(_sort_desc(seq[:half]) + _sort_desc(seq[half:])[::-1])


def _merge_top(a, b, cap):
    n = len(a)
    if 2 * n <= cap:
        return _bitonic_merge(a + b[::-1])
    return _bitonic_merge([jnp.maximum(a[k], b[n - 1 - k]) for k in range(n)])


def _merge_sublanes(s, cap):
    d = SUBLANES // 2
    while d >= 1:
        s = _merge_top(s, [pltpu.roll(v, SUBLANES - d, 0) for v in s], cap)
        d //= 2
    return s


def _top_sorted(x, cap):
    streams = [x[SUBLANES * k:SUBLANES * (k + 1)] for k in range(x.shape[0] // SUBLANES)]
    return _merge_sublanes(_sort_desc(streams)[:cap], cap)


def _peer_kernel(x_ref, sc_ref, sh_ref, g_ref, wqh_ref, wql_ref, k3_ref, u_ref, v_ref,
                 lg_ref, lb_ref, y_ref,
                 hb_ref, q3_ref, st_ref, e_ref, tau_ref, acc_ref):
    bb, tt, d = x_ref.shape
    tn = bb * tt
    nk = st_ref.shape[1]
    eb = u_ref.shape[0] // nk
    ei = pl.program_id(2)

    @pl.when(ei == 0)
    def _():
        h2 = (x_ref[...] * (1.0 + sc_ref[...]) + sh_ref[...]).reshape(tn, d)
        hh, hl = _split(h2)
        hb_ref[...] = hh
        acc_ref[...] = jnp.zeros_like(acc_ref)

        sub = lax.broadcasted_iota(I32, (SUBLANES, tn), 0)

        def pack(rows):
            out = jnp.broadcast_to(rows[0], (SUBLANES, tn))
            for k in range(1, SUBLANES):
                out = jnp.where(sub == k, jnp.broadcast_to(rows[k], (SUBLANES, tn)), out)
            return out

        q_all = _dot3s(hh, hl, wqh_ref[...], wql_ref[...])
        dk = q_all.shape[1] // (2 * PEER_HEADS)
        for hp in range(2 * PEER_HEADS):
            qh, ql = _split(q_all[:, hp * dk:(hp + 1) * dk])
            q3_ref[hp] = jnp.concatenate([qh, ql, qh], 1)

        def per_head(h, carry):
            sv = []
            for p in range(2):
                hp = 2 * h + p
                s_t = _dot(k3_ref[hp], q3_ref[hp], NT)
                st_ref[hp] = s_t
                sv.append([v[0:1] for v in _top_sorted(s_t, PEER_TOPK)])
            sv0, sv1 = sv
            m0, m1 = sv0[0], sv1[0]
            groups = []
            for g in range(PEER_TOPK // SUBLANES):
                base = pack(sv0[SUBLANES * g:SUBLANES * (g + 1)])
                groups.append([base + sv1[b] for b in range(PEER_TOPK)])
            top = groups[0]
            for g in range(1, len(groups)):
                top = _merge_top(top, groups[g], PEER_TOPK)
            top = [v[0:1] for v in _merge_sublanes(top, PEER_TOPK)]
            mtot = m0 + m1
            z = jnp.exp(top[0] - mtot)
            for r in range(1, PEER_TOPK):
                z = z + jnp.exp(top[r] - mtot)
            tau_ref[pl.ds(h, 1), :] = top[PEER_TOPK - 1]
            inv_z = 1.0 / z
            e_ref[2 * h, :, 0:tn] = jnp.exp(st_ref[2 * h] - m0)
            e_ref[2 * h + 1, :, 0:tn] = jnp.exp(st_ref[2 * h + 1] - m1) * inv_z
            return carry

        lax.fori_loop(0, PEER_HEADS, per_head, 0)

    act = _gelu(_dot(u_ref[...], hb_ref[...], NT))
    group = 2 if eb % 2 == 0 else 1
    cw = LANE if tn % LANE == 0 else tn
    ws = [[None] * (tn // cw) for _ in range(eb)]
    for e0i in range(0, eb, group):
        tiles = [ei * eb + e0i + k for k in range(group)]
        for c in range(tn // cw):
            cs = slice(c * cw, (c + 1) * cw)
            gates = [None] * group
            for h in range(PEER_HEADS):
                s1 = st_ref[2 * h + 1, :, cs]
                e1 = e_ref[2 * h + 1, :, cs]
                tau = tau_ref[h:h + 1, cs]
                for k in range(group):
                    s0 = st_ref[2 * h, pl.ds(tiles[k], 1), :][:, cs]
                    e0 = e_ref[2 * h, pl.ds(tiles[k], 1), 0:tn][:, cs]
                    term = jnp.where(s1 + s0 >= tau, e1 * e0, 0.0)
                    gates[k] = term if gates[k] is None else gates[k] + term
            for k in range(group):
                arow = slice((e0i + k) * nk, (e0i + k + 1) * nk)
                ws[e0i + k][c] = (gates[k] * act[arow, cs]).astype(BF16)
    w_all = jnp.concatenate([jnp.concatenate(row, 1) for row in ws], 0)
    acc_ref[...] += _dot(w_all, v_ref[...], TN)

    @pl.when(ei == pl.num_programs(2) - 1)
    def _():
        y = ALPHA * x_ref[...] + g_ref[...] * acc_ref[...].reshape(bb, tt, d)
        y_ref[...] = _layer_norm(y, lg_ref[...], lb_ref[...])


def _peer(x, mod, layer, wq_hi, wq_lo, k3, u_bf, v_bf, ln_g, ln_b, bb, tt, eb):
    b, t, d = x.shape
    nt = t // tt
    tn = bb * tt
    nhp = 2 * PEER_HEADS
    dk = wq_hi.shape[-1] // nhp
    nk = k3.shape[2]
    xspec = pl.BlockSpec((bb, tt, d), lambda bi, ti, ei: (bi, ti, 0))
    vec = pl.BlockSpec((1, d), lambda bi, ti, ei: (0, 0))

    def mod_spec(j):
        return pl.BlockSpec((None, None, bb, 1, d), lambda bi, ti, ei: (layer, j, bi, 0, 0))

    ne = nk // eb
    const = dict(pipeline_mode=pl.Buffered(1))
    wq_spec = pl.BlockSpec((None, d, nhp * dk), lambda bi, ti, ei: (layer, 0, 0), **const)
    k_spec = pl.BlockSpec((None, nhp, nk, 3 * dk), lambda bi, ti, ei: (layer, 0, 0, 0), **const)
    tab_spec = pl.BlockSpec((None, eb * nk, d), lambda bi, ti, ei: (layer, ei, 0))
    return pl.pallas_call(
        _peer_kernel,
        grid=(b // bb, nt, ne),
        in_specs=[xspec, mod_spec(4), mod_spec(3), mod_spec(5), wq_spec, wq_spec, k_spec,
                  tab_spec, tab_spec, vec, vec],
        out_specs=xspec,
        out_shape=jax.ShapeDtypeStruct((b, t, d), F32),
        scratch_shapes=[pltpu.VMEM((tn, d), BF16), pltpu.VMEM((nhp, tn, 3 * dk), BF16),
                        pltpu.VMEM((nhp, nk, tn), F32), pltpu.VMEM((nhp, nk, tn + LANE), F32),
                        pltpu.VMEM((PEER_HEADS, tn), F32),
                        pltpu.VMEM((tn, d), F32)],
        compiler_params=_cparams("arbitrary", "arbitrary", "arbitrary"),
        name="peer",
    )(x, mod, mod, mod, wq_hi, wq_lo, k3, u_bf, v_bf, ln_g, ln_b)


def _block_diag(w):
    l, nb, bw, _ = w.shape
    eye = jnp.eye(nb, dtype=w.dtype)
    return jnp.einsum("lncd,nm->lncmd", w, eye).reshape(l, nb * bw, nb * bw)


def _pick_tile(t, target):
    tt = min(t, target)
    while t % tt:
        tt //= 2
    return tt


def kernel(x_prompt, x_sample, cache_k, cache_v, cache_idx_k, state_conv, state_lru, page_table,
           c_prompt, c_sample, w_ada, b_ada, w_in, conv_w, conv_b, lru_wa, lru_ba, lru_wx, lru_bx,
           lru_lambda, beta_lru, beta_attn, w_out, ln1_g, ln1_b, ln2_g, ln2_b, peer_wq, peer_keys,
           peer_u, peer_v):
    depth = w_ada.shape[0]
    bp, tp, d = x_prompt.shape
    bs, ts, _ = x_sample.shape
    lw = d // 2
    aw = N_HEADS * HEAD_DIM
    page = cache_k.shape[2]
    past = page_table.shape[1] * page
    assert tp >= CONV_W - 1 and ts >= CONV_W - 1 and ts % 8 == 0 and bs % 8 == 0
    assert cache_k.shape[3:] == (N_HEADS, HEAD_DIM) and page >= ts and HEAD_DIM == 64

    in_cols = w_in.shape[-1]
    cols_pad = -(-in_cols // LANE) * LANE
    w_in_hi, w_in_lo = _split(jnp.pad(w_in, ((0, 0), (0, 0), (0, cols_pad - in_cols))))
    w_out_hi, w_out_lo = _split(w_out)
    wa_hi, wa_lo = _split(_block_diag(lru_wa))
    wx_hi, wx_lo = _split(_block_diag(lru_wx))
    nhp = 2 * PEER_HEADS
    dk = peer_wq.shape[-1] // nhp
    wq_hi, wq_lo = _split(peer_wq)
    nk = peer_keys.shape[3]
    pk = peer_keys.transpose(0, 2, 1, 3, 4).reshape(depth, nhp, nk, dk)
    pk_hi, pk_lo = _split(pk)
    pk3 = jnp.concatenate([pk_hi, pk_hi, pk_lo], -1)
    u_bf = peer_u.astype(BF16)
    v_bf = peer_v.astype(BF16)
    eb = 4 if nk % 4 == 0 else 1
    pool_k = cache_k.transpose(0, 1, 3, 4, 2)
    pool_v = cache_v.transpose(0, 1, 3, 4, 2)
    pool_ki = cache_idx_k.transpose(0, 1, 3, 2)

    nb_all = bp + bs
    nb_pad = -(-nb_all // 8) * 8
    c_all = jnp.pad(jnp.concatenate([c_prompt, c_sample], 0), ((0, nb_pad - nb_all), (0, 0)))
    mod = _ada_mod(c_all, w_ada, b_ada)
    mod_p = mod[:, :, :bp].reshape(depth, 6, bp, 1, d)
    mod_s = mod[:, :, bp:nb_all].reshape(depth, 6, bs, 1, d)

    tt_p = _pick_tile(tp, 512)
    tabs_p = [a.reshape(tp // tt_p, tt_p, LANE) for a in _rope_tables(jnp.arange(tp, dtype=I32))]
    bb_s = _pick_tile(bs, 32)
    tabs_s = [jnp.tile(a, (bb_s, 1)).reshape(1, bb_s * ts, LANE)
              for a in _rope_tables(past + jnp.arange(ts, dtype=I32))]

    tt_lru = _pick_tile(tp, 256)
    tq = _pick_tile(tp, Q_BLOCK)
    n_ranges = max(r for r in (1, 2, 4, 8) if (tp // tq) % r == 0)
    tn_p = _pick_tile(tp, 512)
    bb_peer = _pick_tile(bs, max(1, 512 // ts))

    yp, ys = x_prompt, x_sample
    outs = {k: [] for k in ("kp", "vp", "kip", "cp", "hp", "ks", "vs", "kis", "cs", "hs")}
    for l in range(depth):
        lru_w = (conv_w[l], conv_b[l][None], wa_hi[l], wa_lo[l], lru_ba[l][None], wx_hi[l], wx_lo[l],
                 lru_bx[l][None], lru_lambda[l][None], beta_lru[l][None])
        zx, zg, q, k, v, qi, tail, ki, ki3, k16, v16 = _in_proj(yp, mod_p, l, w_in_hi, w_in_lo,
                                                                tabs_p, 1, tt_p)
        zc = jnp.zeros((bp, CONV_W - 1, lw), F32)
        zh = jnp.zeros((bp, lw), F32)
        mix_lru, nconv, nh = _lru_prompt(zx, zg, zc, zh, lru_w, bp, tp, tt_lru)
        o = _attn_prompt(qi, tail, ki3, q, k16, v16, bp, tp, tq, n_ranges)
        x1 = _out_proj(yp, mod_p, l, mix_lru, o, beta_attn[l][None], w_out_hi, w_out_lo,
                       ln1_g[l][None], ln1_b[l][None], 1, tt_p)
        yp = _peer(x1, mod_p, l, wq_hi, wq_lo, pk3, u_bf, v_bf,
                   ln2_g[l][None], ln2_b[l][None], 1, tn_p, eb)
        outs["kp"].append(k.reshape(bp, tp, N_HEADS, HEAD_DIM))
        outs["vp"].append(v.reshape(bp, tp, N_HEADS, HEAD_DIM))
        outs["kip"].append(ki.reshape(bp, tp, IDX_DIM))
        outs["cp"].append(nconv)
        outs["hp"].append(nh.reshape(bp, lw))
        zx, zg, q, k, v, qi, tail, ki = _in_proj(ys, mod_s, l, w_in_hi, w_in_lo, tabs_s, bb_s, ts)[:8]
        tm = lambda a: a.reshape(bs, ts, lw).transpose(1, 0, 2)
        mix_t, nconv_t, nh = _lru_sample(tm(zx), tm(zg), state_conv[l].transpose(1, 0, 2),
                                         state_lru[l], lru_w)
        mix_lru = mix_t.transpose(1, 0, 2).reshape(bs * ts, lw)
        o = _attn_sample(page_table, qi, tail, q, k, v, pool_ki, pool_k, pool_v, l, bs, ts)
        x1 = _out_proj(ys, mod_s, l, mix_lru, o, beta_attn[l][None], w_out_hi, w_out_lo,
                       ln1_g[l][None], ln1_b[l][None], bb_s, ts)
        ys = _peer(x1, mod_s, l, wq_hi, wq_lo, pk3, u_bf, v_bf,
                   ln2_g[l][None], ln2_b[l][None], bb_peer, ts, eb)
        outs["ks"].append(k.reshape(bs, ts, N_HEADS, HEAD_DIM))
        outs["vs"].append(v.reshape(bs, ts, N_HEADS, HEAD_DIM))
        outs["kis"].append(ki.reshape(bs, ts, IDX_DIM))
        outs["cs"].append(nconv_t.transpose(1, 0, 2))
        outs["hs"].append(nh)

    st = lambda name: jnp.stack(outs[name])
    return (yp, ys, st("kp"), st("vp"), st("kip"), st("cp"), st("hp"),
            st("ks"), st("vs"), st("kis"), st("cs"), st("hs"))
```

```python
import functools
import math

import jax
import jax.numpy as jnp
from jax import lax
from jax.experimental import pallas as pl
from jax.experimental.pallas import tpu as pltpu

F32 = jnp.float32
BF16 = jnp.bfloat16
I32 = jnp.int32

LRU_C = 8.0
CONV_W = 4
N_HEADS = 8
HEAD_DIM = 64
IDX_HEADS = 4
IDX_DIM = 64
TOPK_MAX = 256
ROPE_THETA = 500000.0
ROPE_FRAC_DIV = 4
Q_BLOCK = 128
PEER_HEADS = 8
PEER_TOPK = 16
DEPTH_NOMINAL = 4
ALPHA = (2 * DEPTH_NOMINAL) ** 0.25
LN_EPS = 1e-5

LANE = 128
VMEM_LIMIT = 56 * 1024 * 1024

NN = (((1,), (0,)), ((), ()))
NT = (((1,), (1,)), ((), ()))
TN = (((0,), (0,)), ((), ()))

NEG_INF = float("-inf")
INT_MIN = -2147483648
KEY_NEG_INF = -2139095041


def _cparams(*sem):
    return pltpu.CompilerParams(dimension_semantics=sem, vmem_limit_bytes=VMEM_LIMIT)


def _split(a):
    hi = a.astype(BF16)
    lo = (a - hi.astype(F32)).astype(BF16)
    return hi, lo


def _dot(a, b, dims=NN):
    return lax.dot_general(a, b, dims, preferred_element_type=F32)


def _dot3s(ah, al, bh, bl, dims=NN):
    return _dot(ah, bh, dims) + (_dot(ah, bl, dims) + _dot(al, bh, dims))


def _dot3w(a, bh, bl, dims=NN):
    ah, al = _split(a)
    return _dot3s(ah, al, bh, bl, dims)


def _cat3(x, lhs):
    hi = x.astype(BF16).astype(F32)
    lo = (x - hi).astype(BF16).astype(F32)
    parts = [hi, hi, lo] if lhs else [hi, lo, hi]
    return jnp.concatenate(parts, 1).astype(BF16)


def _gelu(x):
    return 0.5 * x * (1.0 + lax.erf(x * (1.0 / math.sqrt(2.0))))


def _layer_norm(y, g, b):
    mu = jnp.mean(y, -1, keepdims=True)
    d = y - mu
    var = jnp.mean(d * d, -1, keepdims=True)
    return d * lax.rsqrt(var + LN_EPS) * g + b


def _rms(y, g):
    return y * lax.rsqrt(jnp.mean(y * y, -1, keepdims=True) + LN_EPS) * g


def _order_key(x):
    b = pltpu.bitcast(x, I32)
    return jnp.where(b < 0, b ^ 0x7FFFFFFF, b)


def _ada_kernel(c_ref, w_ref, b_ref, o_ref):
    c = c_ref[...]
    s = c * jax.nn.sigmoid(c)
    wh, wl = _split(w_ref[...])
    o_ref[...] = _dot3w(s, wh, wl) + b_ref[...]


def _ada_mod(c_all, w_ada, b_ada):
    depth, d, six_d = w_ada.shape
    nb = c_all.shape[0]
    return pl.pallas_call(
        _ada_kernel,
        grid=(depth, six_d // d),
        in_specs=[
            pl.BlockSpec((nb, d), lambda l, j: (0, 0)),
            pl.BlockSpec((None, d, d), lambda l, j: (l, 0, j)),
            pl.BlockSpec((None, None, 1, d), lambda l, j: (l, j, 0, 0)),
        ],
        out_specs=pl.BlockSpec((None, None, nb, d), lambda l, j: (l, j, 0, 0)),
        out_shape=jax.ShapeDtypeStruct((depth, six_d // d, nb, d), F32),
        compiler_params=_cparams("arbitrary", "arbitrary"),
        name="ada_mod",
    )(c_all, w_ada, b_ada.reshape(depth, six_d // d, 1, d))


def _rope(x, c, s1, s2):
    w = x.shape[-1]
    return x * c + pltpu.roll(x, 8, 1) * s1 + pltpu.roll(x, w - 8, 1) * s2


def _inproj_kernel(x_ref, sc_ref, sh_ref, wh_ref, wl_ref, rc_ref, rs1_ref, rs2_ref,
                   tc_ref, ts1_ref, ts2_ref,
                   zx_ref, zg_ref, q_ref, k_ref, v_ref, qi_ref, tail_ref, ki_ref, ki3_ref,
                   kb_ref, vb_ref):
    bb, tt, d = x_ref.shape
    h = x_ref[...] * (1.0 + sc_ref[...]) + sh_ref[...]
    hh, hl = _split(h.reshape(bb * tt, d))

    def proj(lo, hi):
        return _dot3s(hh, hl, wh_ref[:, lo:hi], wl_ref[:, lo:hi])

    def proj1(lo, hi):
        return _dot(hh, wh_ref[:, lo:hi])

    lw = zx_ref.shape[-1]
    aw = q_ref.shape[-1]
    iw = qi_ref.shape[-1]
    o = 0
    zx_ref[...] = proj1(o, o + lw); o += lw
    zg_ref[...] = proj1(o, o + lw); o += lw
    rc, rs1, rs2 = rc_ref[...], rs1_ref[...], rs2_ref[...]
    rep = aw // LANE
    c4 = jnp.concatenate([rc] * rep, 1)
    s14 = jnp.concatenate([rs1] * rep, 1)
    s24 = jnp.concatenate([rs2] * rep, 1)
    q_ref[...] = _rope(proj1(o, o + aw), c4, s14, s24); o += aw
    kk = _rope(proj1(o, o + aw), c4, s14, s24); o += aw
    k_ref[...] = kk
    kb_ref[...] = kk.astype(BF16)
    vv = proj1(o, o + aw); o += aw
    v_ref[...] = vv
    vb_ref[...] = vv.astype(BF16)
    repi = iw // LANE
    qi_ref[...] = _rope(proj(o, o + iw), jnp.concatenate([rc] * repi, 1),
                        jnp.concatenate([rs1] * repi, 1), jnp.concatenate([rs2] * repi, 1)); o += iw
    tail = _rope(proj(o, o + LANE), tc_ref[...], ts1_ref[...], ts2_ref[...])
    tail_ref[...] = tail
    ki_ref[...] = tail[:, :IDX_DIM]
    ki3_ref[...] = _cat3(tail[:, :IDX_DIM], lhs=False)


def _rope_tables(pos):
    t = pos.shape[0]
    rot = HEAD_DIM // ROPE_FRAC_DIV
    half = rot // 2
    freqs = ROPE_THETA ** (-jnp.arange(half, dtype=F32) / half)
    ang = pos.astype(F32)[:, None] * freqs[None, :]
    cos, sin = jnp.cos(ang), jnp.sin(ang)
    ones = jnp.ones((t, HEAD_DIM - rot), F32)
    zeros = jnp.zeros((t, HEAD_DIM - rot), F32)
    zh = jnp.zeros((t, half), F32)
    c64 = jnp.concatenate([cos, cos, ones], 1)
    s1_64 = jnp.concatenate([zh, sin, zeros], 1)
    s2_64 = jnp.concatenate([-sin, zh, zeros], 1)
    rc = jnp.concatenate([c64, c64], 1)
    rs1 = jnp.concatenate([s1_64, s1_64], 1)
    rs2 = jnp.concatenate([s2_64, s2_64], 1)
    wscale = jnp.concatenate([jnp.full((t, IDX_HEADS), IDX_HEADS ** -0.5, F32),
                              jnp.ones((t, LANE - IDX_DIM - IDX_HEADS), F32)], 1)
    z64 = jnp.zeros((t, LANE - IDX_DIM), F32)
    tc = jnp.concatenate([c64, wscale], 1)
    ts1 = jnp.concatenate([s1_64, z64], 1)
    ts2 = jnp.concatenate([s2_64, z64], 1)
    return rc, rs1, rs2, tc, ts1, ts2


def _in_proj(x, mod, layer, w_hi, w_lo, tables, bb, tt):
    b, t, d = x.shape
    n = b * t
    rows = bb * tt
    nt = t // tt
    lw = d // 2
    aw = N_HEADS * HEAD_DIM
    iw = IDX_HEADS * IDX_DIM
    cols = w_hi.shape[-1]
    tab_idx = (lambda bi, ti: (ti, 0, 0)) if tables[0].shape[0] > 1 else (lambda bi, ti: (0, 0, 0))
    tab_spec = pl.BlockSpec((None, rows, LANE), tab_idx)
    row_idx = lambda bi, ti: (bi * nt + ti, 0)

    def mod_spec(j):
        return pl.BlockSpec((None, None, bb, 1, d), lambda bi, ti: (layer, j, bi, 0, 0))

    outs = [(n, lw), (n, lw), (n, aw), (n, aw), (n, aw), (n, iw), (n, LANE), (n, IDX_DIM),
            (n, 3 * IDX_DIM), (n, aw), (n, aw)]
    dtypes = [F32] * 8 + [BF16] * 3
    return pl.pallas_call(
        _inproj_kernel,
        grid=(b // bb, nt),
        in_specs=[
            pl.BlockSpec((bb, tt, d), lambda bi, ti: (bi, ti, 0)),
            mod_spec(1), mod_spec(0),
            pl.BlockSpec((None, d, cols), lambda bi, ti: (layer, 0, 0)),
            pl.BlockSpec((None, d, cols), lambda bi, ti: (layer, 0, 0)),
        ] + [tab_spec] * 6,
        out_specs=[pl.BlockSpec((rows, w), row_idx) for _, w in outs],
        out_shape=[jax.ShapeDtypeStruct(s, dt) for s, dt in zip(outs, dtypes)],
        compiler_params=_cparams("arbitrary", "arbitrary"),
        name="in_proj",
    )(x, mod, mod, w_hi, w_lo, *tables)


def _lru_gates(xc, wah, wal, ba, wxh, wxl, bx, lam):
    xh, xl = _split(xc)
    r = jax.nn.sigmoid(_dot3s(xh, xl, wah, wal) + ba)
    i = jax.nn.sigmoid(_dot3s(xh, xl, wxh, wxl) + bx)
    nl = -lam
    sp = jnp.maximum(nl, 0.0) + jnp.log1p(jnp.exp(-jnp.abs(nl)))
    log_a = -LRU_C * r * sp
    a = jnp.exp(log_a)
    u = jnp.sqrt(1.0 - a * a) * (i * xc)
    return a, u


def _lru_prompt_kernel(zx_ref, zg_ref, cbuf_ref, h0_ref, cw_ref, cb_ref, wah_ref, wal_ref, ba_ref,
                       wxh_ref, wxl_ref, bx_ref, lam_ref, beta_ref,
                       mix_ref, nconv_ref, nh_ref, ext_ref, hst_ref):
    tt = zx_ref.shape[0]
    ti = pl.program_id(1)

    @pl.when(ti == 0)
    def _():
        ext_ref[5:8, :] = cbuf_ref[...]
        hst_ref[0:1, :] = h0_ref[...]

    x = zx_ref[...]
    ext_ref[8:8 + tt, :] = x
    w = cw_ref[...]
    xc = (cb_ref[...] + ext_ref[5:5 + tt, :] * w[0:1] + ext_ref[6:6 + tt, :] * w[1:2]
          + ext_ref[7:7 + tt, :] * w[2:3] + x * w[3:4])
    tail3 = ext_ref[tt + 5:tt + 8, :]
    ext_ref[5:8, :] = tail3

    a, u = _lru_gates(xc, wah_ref[...], wal_ref[...], ba_ref[...], wxh_ref[...], wxl_ref[...],
                      bx_ref[...], lam_ref[...])
    row = lax.broadcasted_iota(I32, (tt, 1), 0)
    s = 1
    while s < tt:
        a_sh = pltpu.roll(a, s, 0)
        u_sh = pltpu.roll(u, s, 0)
        m = row >= s
        u = u + a * jnp.where(m, u_sh, 0.0)
        a = a * jnp.where(m, a_sh, 1.0)
        s *= 2
    h = a * hst_ref[0:1, :] + u
    hst_ref[0:1, :] = h[tt - 1:tt, :]
    y = _gelu(zg_ref[...]) * h
    mix_ref[...] = _rms(y, beta_ref[...])

    @pl.when(ti == pl.num_programs(1) - 1)
    def _():
        nconv_ref[...] = tail3
        nh_ref[...] = h[tt - 1:tt, :]


def _lru_prompt(zx, zg, cbuf, h0, lw, b, t, tt):
    n, w = zx.shape
    nt = t // tt
    row_spec = pl.BlockSpec((tt, w), lambda bi, ti: (bi * nt + ti, 0))
    full = lambda shape: pl.BlockSpec(shape, lambda bi, ti: tuple(0 for _ in shape))
    vec = full((1, w))
    mat = full((w, w))
    return pl.pallas_call(
        _lru_prompt_kernel,
        grid=(b, nt),
        in_specs=[row_spec, row_spec,
                  pl.BlockSpec((None, CONV_W - 1, w), lambda bi, ti: (bi, 0, 0)),
                  pl.BlockSpec((None, 1, w), lambda bi, ti: (bi, 0, 0)),
                  full((CONV_W, w)), vec, mat, mat, vec, mat, mat, vec, vec, vec],
        out_specs=[row_spec,
                   pl.BlockSpec((None, CONV_W - 1, w), lambda bi, ti: (bi, 0, 0)),
                   pl.BlockSpec((None, 1, w), lambda bi, ti: (bi, 0, 0))],
        out_shape=[jax.ShapeDtypeStruct((n, w), F32),
                   jax.ShapeDtypeStruct((b, CONV_W - 1, w), F32),
                   jax.ShapeDtypeStruct((b, 1, w), F32)],
        scratch_shapes=[pltpu.VMEM((tt + 8, w), F32), pltpu.VMEM((8, w), F32)],
        compiler_params=_cparams("arbitrary", "arbitrary"),
        name="lru_prompt",
    )(zx, zg, cbuf, h0.reshape(b, 1, w), *lw)


def _lru_sample_kernel(zx_ref, zg_ref, cbuf_ref, h0_ref, cw_ref, cb_ref, wah_ref, wal_ref, ba_ref,
                       wxh_ref, wxl_ref, bx_ref, lam_ref, beta_ref,
                       mix_ref, nconv_ref, nh_ref, xc_ref):
    t, b, wd = zx_ref.shape
    w = cw_ref[...]
    xp = [cbuf_ref[k] for k in range(CONV_W - 1)] + [zx_ref[k] for k in range(t)]
    for k in range(t):
        xc_ref[k * b:(k + 1) * b, :] = (cb_ref[...] + xp[k] * w[0:1] + xp[k + 1] * w[1:2]
                                        + xp[k + 2] * w[2:3] + xp[k + 3] * w[3:4])
    for k in range(CONV_W - 1):
        nconv_ref[k] = xp[t + k]
    a, u = _lru_gates(xc_ref[...], wah_ref[...], wal_ref[...], ba_ref[...], wxh_ref[...],
                      wxl_ref[...], bx_ref[...], lam_ref[...])
    h = h0_ref[...]
    beta = beta_ref[...]
    for k in range(t):
        h = a[k * b:(k + 1) * b, :] * h + u[k * b:(k + 1) * b, :]
        mix_ref[k] = _rms(_gelu(zg_ref[k]) * h, beta)
    nh_ref[...] = h


def _lru_sample(zx_t, zg_t, cbuf_t, h0, lw):
    t, b, w = zx_t.shape
    full = lambda shape: pl.BlockSpec(shape, lambda i: tuple(0 for _ in shape))
    vec = full((1, w))
    mat = full((w, w))
    return pl.pallas_call(
        _lru_sample_kernel,
        grid=(1,),
        in_specs=[full((t, b, w)), full((t, b, w)), full((CONV_W - 1, b, w)), full((b, w)),
                  full((CONV_W, w)), vec, mat, mat, vec, mat, mat, vec, vec, vec],
        out_specs=[full((t, b, w)), full((CONV_W - 1, b, w)), full((b, w))],
        out_shape=[jax.ShapeDtypeStruct((t, b, w), F32),
                   jax.ShapeDtypeStruct((CONV_W - 1, b, w), F32),
                   jax.ShapeDtypeStruct((b, w), F32)],
        scratch_shapes=[pltpu.VMEM((t * b, w), F32)],
        compiler_params=_cparams("arbitrary"),
        name="lru_sample",
    )(zx_t, zg_t, cbuf_t, h0, *lw)


def _select_bias(score, qpos, key_ref, topk):
    q, s = score.shape
    kpos = lax.broadcasted_iota(I32, (q, s), 1)
    vis = kpos <= qpos
    key_ref[...] = jnp.where(vis, _order_key(score + 0.0), KEY_NEG_INF)

    kf = float(topk)

    ngrp = 2 if q % (2 * SUBLANES) == 0 else 1
    rg = q // ngrp
    unroll = 4

    def count_ge(g, t):
        keys = key_ref[g * rg:(g + 1) * rg, :]
        return jnp.sum(jnp.where(keys >= t, 1.0, 0.0), axis=1, keepdims=True)

    init = []
    for g in range(ngrp):
        c0 = count_ge(g, jnp.zeros((rg, 1), I32))
        small = (qpos[g * rg:(g + 1) * rg] + 1) <= topk
        init.append(jnp.where(small, KEY_NEG_INF + 1, jnp.where(c0 >= kf, 0, INT_MIN)).astype(I32))
        init.append(jnp.where(small, 1, jnp.where(c0 == kf, 1, 0)).astype(I32))

    def unresolved(state):
        m = state[1]
        for g in range(1, ngrp):
            m = jnp.minimum(m, state[2 * g + 1])
        return jnp.min(m) == 0

    def vcond(c):
        return jnp.logical_and(c[0] < 31, unresolved(c[1:]))

    def vbody(c):
        i, state = c[0], list(c[1:])
        for u in range(unroll):
            bit = jnp.where(i + u < 31, jnp.left_shift(jnp.int32(1), jnp.maximum(30 - i - u, 0)), 0)
            for g in range(ngrp):
                t, res = state[2 * g], state[2 * g + 1]
                cand = t | bit
                cnt = count_ge(g, cand)
                state[2 * g] = jnp.where(res > 0, t, jnp.where(cnt >= kf, cand, t))
                state[2 * g + 1] = jnp.where(cnt == kf, 1, res)
        return (i + unroll, *state)

    out = lax.while_loop(vcond, vbody, (jnp.int32(0), *init))
    thr = jnp.concatenate([out[1 + 2 * g] for g in range(ngrp)], 0)
    res = jnp.concatenate([out[2 + 2 * g] for g in range(ngrp)], 0)
    nbits = max(1, (s - 1).bit_length())

    def tie_phase():
        c_gt = jnp.sum(jnp.where(key_ref[...] > thr, 1.0, 0.0), axis=1, keepdims=True)
        need = kf - c_gt

        def ibody(i, j):
            cand = j + jnp.left_shift(jnp.int32(1), nbits - 1 - i)
            kp = lax.broadcasted_iota(I32, (q, s), 1)
            hit = jnp.where(key_ref[...] == thr, jnp.where(kp <= cand, 1.0, 0.0), 0.0)
            f = jnp.sum(hit, axis=1, keepdims=True)
            return jnp.where(f < need, cand, j)

        return lax.fori_loop(0, nbits, ibody, jnp.full((q, 1), -1, I32)) + 1

    jsel = lax.cond(jnp.min(res) == 0, tie_phase, lambda: jnp.full((q, 1), s, I32))
    jsel = jnp.where(res > 0, s, jsel)
    key = key_ref[...]
    sel = jnp.where(key > thr, 1, jnp.where(key == thr, jnp.where(kpos <= jsel, 1, 0), 0))
    sel = jnp.where(vis, sel, 0)
    return jnp.where(sel > 0, 0.0, NEG_INF)


def _attn_prompt_kernel(qi_ref, tailq_ref, ki3_ref, q_ref, k_ref, v_ref, o_ref, bias_ref, key_ref,
                        *, topk, qb0):
    tq = q_ref.shape[0]
    ki3 = ki3_ref[...]
    tqv = tailq_ref[...]
    qi = qi_ref[...]
    score = None
    for h in range(IDX_HEADS):
        lg = _dot(_cat3(qi[:, h * IDX_DIM:(h + 1) * IDX_DIM], lhs=True), ki3, NT)
        term = tqv[:, IDX_DIM + h:IDX_DIM + h + 1] * jnp.maximum(lg, 0.0)
        score = term if score is None else score + term
    qpos = (qb0 + pl.program_id(1)) * tq + lax.broadcasted_iota(I32, (tq, 1), 0)
    bias_ref[0:tq, :] = _select_bias(score, qpos, key_ref, topk)
    bias_ref[tq:2 * tq, :] = bias_ref[0:tq, :]

    low = lax.broadcasted_iota(I32, (tq, LANE), 1) < HEAD_DIM
    for g in range(q_ref.shape[1] // LANE):
        cs = slice(g * LANE, (g + 1) * LANE)
        qg = q_ref[:, cs] * (HEAD_DIM ** -0.5)
        qpair = jnp.concatenate([jnp.where(low, qg, 0.0), jnp.where(low, 0.0, qg)], 0).astype(BF16)
        s = _dot(qpair, k_ref[:, cs], NT) + bias_ref[...]
        m = jnp.max(s, axis=1, keepdims=True)
        p = jnp.exp(s - m)
        l = jnp.sum(p, axis=1, keepdims=True)
        og = _dot(p.astype(BF16), v_ref[:, cs]) / l
        o_ref[:, cs] = jnp.where(low, og[0:tq], og[tq:2 * tq])


def _attn_prompt(qi, tail, ki3, q, k_bf, v_bf, b, t, tq, n_ranges):
    nq = t // tq
    topk = min(TOPK_MAX, t // 4)
    iw = qi.shape[-1]
    aw = q.shape[-1]
    nqr = nq // n_ranges
    keys3 = lambda a: a.reshape(b, t, a.shape[-1])
    outs = []
    for r in range(n_ranges):
        qb0 = r * nqr
        tv = (r + 1) * nqr * tq
        qrow = lambda w, qb0=qb0: pl.BlockSpec((tq, w), lambda bi, qb: (bi * nq + qb0 + qb, 0))
        kvis = lambda w, tv=tv: pl.BlockSpec((None, tv, w), lambda bi, qb: (bi, 0, 0))
        outs.append(pl.pallas_call(
            functools.partial(_attn_prompt_kernel, topk=topk, qb0=qb0),
            grid=(b, nqr),
            in_specs=[qrow(iw), qrow(LANE), kvis(3 * IDX_DIM), qrow(aw), kvis(aw), kvis(aw)],
            out_specs=pl.BlockSpec((None, tq, aw), lambda bi, qb: (bi, qb, 0)),
            out_shape=jax.ShapeDtypeStruct((b, nqr * tq, aw), F32),
            scratch_shapes=[pltpu.VMEM((2 * tq, tv), F32), pltpu.VMEM((tq, tv), I32)],
            compiler_params=_cparams("arbitrary", "arbitrary"),
            name="attn_prompt",
        )(qi, tail, keys3(ki3), q, keys3(k_bf), keys3(v_bf)))
    return jnp.concatenate(outs, 1).reshape(b * t, aw)


def _attn_sample_kernel(pt_ref, qi_ref, tail_ref, q_ref, kn_ref, vn_ref, *rest, n_pages, page, topk):
    ki_pages = rest[:n_pages]
    k_pages = rest[n_pages:2 * n_pages]
    v_pages = rest[2 * n_pages:3 * n_pages]
    o_ref, kit_ref, kst_ref, vst_ref, key_ref = rest[3 * n_pages:]
    t = q_ref.shape[0]
    past = n_pages * page
    hsl = lambda h: slice(h * HEAD_DIM, (h + 1) * HEAD_DIM)
    for j in range(n_pages):
        cols = slice(j * page, (j + 1) * page)
        kit = ki_pages[j][...]
        hi = kit.astype(BF16)
        lo = (kit - hi.astype(F32)).astype(BF16)
        kit_ref[0:IDX_DIM, cols] = hi
        kit_ref[IDX_DIM:2 * IDX_DIM, cols] = lo
        kit_ref[2 * IDX_DIM:3 * IDX_DIM, cols] = hi
        kst_ref[:, :, cols] = k_pages[j][...].astype(BF16)
        vst_ref[:, :, cols] = v_pages[j][...].astype(BF16)
    tl = tail_ref[...]
    ki_new = _cat3(jnp.concatenate([tl[:, :IDX_DIM], jnp.zeros((page - t, IDX_DIM), F32)], 0),
                   lhs=False)
    kn = kn_ref[...]
    vn = vn_ref[...]
    zpad = jnp.zeros((page - t, HEAD_DIM), F32)

    qi = qi_ref[...]
    qi_rows = jnp.concatenate([qi[:, h * IDX_DIM:(h + 1) * IDX_DIM] for h in range(IDX_HEADS)], 0)
    w_rows = jnp.concatenate([tl[:, IDX_DIM + h:IDX_DIM + h + 1] for h in range(IDX_HEADS)], 0)
    qi3 = _cat3(qi_rows, lhs=True)
    logits = jnp.concatenate([_dot(qi3, kit_ref[...]), _dot(qi3, ki_new, NT)], 1)
    sc = jnp.maximum(logits, 0.0) * w_rows
    score = sc[0:t]
    for h in range(1, IDX_HEADS):
        score = score + sc[h * t:(h + 1) * t]
    qpos = past + lax.broadcasted_iota(I32, (t, 1), 0)
    bias = _select_bias(score, qpos, key_ref, topk)

    q = q_ref[...] * (HEAD_DIM ** -0.5)
    rpad = -t % 16
    outs = []
    for h in range(N_HEADS):
        qh = jnp.concatenate([q[:, hsl(h)], jnp.zeros((rpad, HEAD_DIM), F32)], 0).astype(BF16)
        k_new = jnp.concatenate([kn[:, hsl(h)], zpad], 0).astype(BF16)
        v_new = jnp.concatenate([vn[:, hsl(h)], zpad], 0).astype(BF16)
        s = jnp.concatenate([_dot(qh, kst_ref[h]), _dot(qh, k_new, NT)], 1)[0:t] + bias
        m = jnp.max(s, axis=1, keepdims=True)
        p = jnp.exp(s - m)
        l = jnp.sum(p, axis=1, keepdims=True)
        ph = jnp.concatenate([p, jnp.zeros((rpad, past + page), F32)], 0).astype(BF16)
        o = _dot(ph[:, :past], vst_ref[h], NT) + _dot(ph[:, past:], v_new)
        outs.append(o[0:t] / l)
    o_ref[...] = jnp.concatenate(outs, 1)


def _attn_sample(page_table, qi, tail, q, kn, vn, pool_ki, pool_k, pool_v, layer, b, t):
    n_pages = page_table.shape[1]
    page = pool_ki.shape[3]
    past = n_pages * page
    topk = min(TOPK_MAX, (past + t) // 4)
    aw = q.shape[-1]
    iw = qi.shape[-1]
    s_pad = past + page
    row = lambda w: pl.BlockSpec((t, w), lambda bi, pt: (bi, 0))

    def ki_spec(j):
        return pl.BlockSpec((None, None, IDX_DIM, page), lambda bi, pt: (layer, pt[bi, j], 0, 0))

    def kv_spec(j):
        return pl.BlockSpec((None, None, N_HEADS, HEAD_DIM, page),
                            lambda bi, pt: (layer, pt[bi, j], 0, 0, 0))

    in_specs = [row(iw), row(LANE), row(aw), row(aw), row(aw)]
    in_specs += [ki_spec(j) for j in range(n_pages)]
    in_specs += [kv_spec(j) for j in range(n_pages)]
    in_specs += [kv_spec(j) for j in range(n_pages)]
    grid_spec = pltpu.PrefetchScalarGridSpec(
        num_scalar_prefetch=1,
        grid=(b,),
        in_specs=in_specs,
        out_specs=row(aw),
        scratch_shapes=[pltpu.VMEM((3 * IDX_DIM, past), BF16),
                        pltpu.VMEM((N_HEADS, HEAD_DIM, past), BF16),
                        pltpu.VMEM((N_HEADS, HEAD_DIM, past), BF16),
                        pltpu.VMEM((t, s_pad), I32)],
    )
    return pl.pallas_call(
        functools.partial(_attn_sample_kernel, n_pages=n_pages, page=page, topk=topk),
        grid_spec=grid_spec,
        out_shape=jax.ShapeDtypeStruct((b * t, aw), F32),
        compiler_params=_cparams("arbitrary"),
        name="attn_sample",
    )(page_table, qi, tail, q, kn, vn, *([pool_ki] * n_pages), *([pool_k] * n_pages),
      *([pool_v] * n_pages))


def _outproj_kernel(x_ref, g_ref, ml_ref, o_ref, beta_ref, wh_ref, wl_ref, lg_ref, lb_ref, y_ref):
    bb, tt, d = x_ref.shape
    lw = ml_ref.shape[-1]
    mo = _rms(o_ref[...], beta_ref[...])
    acc = (_dot3w(ml_ref[...], wh_ref[0:lw, :], wl_ref[0:lw, :])
           + _dot3w(mo, wh_ref[lw:, :], wl_ref[lw:, :]))
    y = ALPHA * x_ref[...] + g_ref[...] * acc.reshape(bb, tt, d)
    y_ref[...] = _layer_norm(y, lg_ref[...], lb_ref[...])


def _out_proj(x, mod, layer, mix_lru, o, beta_attn, w_hi, w_lo, ln_g, ln_b, bb, tt):
    b, t, d = x.shape
    nt = t // tt
    rows = bb * tt
    lw = mix_lru.shape[-1]
    aw = o.shape[-1]
    xspec = pl.BlockSpec((bb, tt, d), lambda bi, ti: (bi, ti, 0))
    vec = lambda w: pl.BlockSpec((1, w), lambda bi, ti: (0, 0))
    wspec = pl.BlockSpec((None, lw + aw, d), lambda bi, ti: (layer, 0, 0))
    return pl.pallas_call(
        _outproj_kernel,
        grid=(b // bb, nt),
        in_specs=[xspec,
                  pl.BlockSpec((None, None, bb, 1, d), lambda bi, ti: (layer, 2, bi, 0, 0)),
                  pl.BlockSpec((rows, lw), lambda bi, ti: (bi * nt + ti, 0)),
                  pl.BlockSpec((rows, aw), lambda bi, ti: (bi * nt + ti, 0)),
                  vec(aw), wspec, wspec, vec(d), vec(d)],
        out_specs=xspec,
        out_shape=jax.ShapeDtypeStruct((b, t, d), F32),
        compiler_params=_cparams("arbitrary", "arbitrary"),
        name="out_proj",
    )(x, mod, mix_lru, o, beta_attn, w_hi, w_lo, ln_g, ln_b)


SUBLANES = 8


def _bitonic_merge(seq):
    seq = list(seq)
    n = len(seq)
    d = n // 2
    while d >= 1:
        for i in range(n):
            if i & d == 0:
                a, b = seq[i], seq[i + d]
                seq[i], seq[i + d] = jnp.maximum(a, b), jnp.minimum(a, b)
        d //= 2
    return seq


def _sort_desc(seq):
    if len(seq) == 1:
        return list(seq)
    half = len(seq) // 2
    return _bitonic_merge(_sort_desc(seq[:half]) + _sort_desc(seq[half:])[::-1])


def _merge_top(a, b, cap):
    n = len(a)
    if 2 * n <= cap:
        return _bitonic_merge(a + b[::-1])
    return _bitonic_merge([jnp.maximum(a[k], b[n - 1 - k]) for k in range(n)])


def _merge_sublanes(s, cap):
    d = SUBLANES // 2
    while d >= 1:
        s = _merge_top(s, [pltpu.roll(v, SUBLANES - d, 0) for v in s], cap)
        d //= 2
    return s


def _top_sorted(x, cap):
    streams = [x[SUBLANES * k:SUBLANES * (k + 1)] for k in range(x.shape[0] // SUBLANES)]
    return _merge_sublanes(_sort_desc(streams)[:cap], cap)


def _peer_kernel(x_ref, sc_ref, sh_ref, g_ref, wqh_ref, wql_ref, k3_ref, u_ref, v_ref,
                 lg_ref, lb_ref, y_ref,
                 hb_ref, q3_ref, st_ref, e_ref, tau_ref, acc_ref):
    bb, tt, d = x_ref.shape
    tn = bb * tt
    nk = st_ref.shape[1]
    eb = u_ref.shape[0] // nk
    ei = pl.program_id(2)

    @pl.when(ei == 0)
    def _():
        h2 = (x_ref[...] * (1.0 + sc_ref[...]) + sh_ref[...]).reshape(tn, d)
        hh, hl = _split(h2)
        hb_ref[...] = hh
        acc_ref[...] = jnp.zeros_like(acc_ref)

        sub = lax.broadcasted_iota(I32, (SUBLANES, tn), 0)

        def pack(rows):
            out = jnp.broadcast_to(rows[0], (SUBLANES, tn))
            for k in range(1, SUBLANES):
                out = jnp.where(sub == k, jnp.broadcast_to(rows[k], (SUBLANES, tn)), out)
            return out

        q_all = _dot3s(hh, hl, wqh_ref[...], wql_ref[...])
        dk = q_all.shape[1] // (2 * PEER_HEADS)
        for hp in range(2 * PEER_HEADS):
            qh, ql = _split(q_all[:, hp * dk:(hp + 1) * dk])
            q3_ref[hp] = jnp.concatenate([qh, ql, qh], 1)

        def per_head(h, carry):
            sv = []
            for p in range(2):
                hp = 2 * h + p
                s_t = _dot(k3_ref[hp], q3_ref[hp], NT)
                st_ref[hp] = s_t
                sv.append([v[0:1] for v in _top_sorted(s_t, PEER_TOPK)])
            sv0, sv1 = sv
            m0, m1 = sv0[0], sv1[0]
            groups = []
            for g in range(PEER_TOPK // SUBLANES):
                base = pack(sv0[SUBLANES * g:SUBLANES * (g + 1)])
                groups.append([base + sv1[b] for b in range(PEER_TOPK)])
            top = groups[0]
            for g in range(1, len(groups)):
                top = _merge_top(top, groups[g], PEER_TOPK)
            top = [v[0:1] for v in _merge_sublanes(top, PEER_TOPK)]
            mtot = m0 + m1
            z = jnp.exp(top[0] - mtot)
            for r in range(1, PEER_TOPK):
                z = z + jnp.exp(top[r] - mtot)
            tau_ref[pl.ds(h, 1), :] = top[PEER_TOPK - 1]
            inv_z = 1.0 / z
            e_ref[2 * h, :, 0:tn] = jnp.exp(st_ref[2 * h] - m0)
            e_ref[2 * h + 1, :, 0:tn] = jnp.exp(st_ref[2 * h + 1] - m1) * inv_z
            return carry

        lax.fori_loop(0, PEER_HEADS, per_head, 0)

    act = _gelu(_dot(u_ref[...], hb_ref[...], NT))
    group = 2 if eb % 2 == 0 else 1
    cw = LANE if tn % LANE == 0 else tn
    ws = [[None] * (tn // cw) for _ in range(eb)]
    for e0i in range(0, eb, group):
        tiles = [ei * eb + e0i + k for k in range(group)]
        for c in range(tn // cw):
            cs = slice(c * cw, (c + 1) * cw)
            gates = [None] * group
            for h in range(PEER_HEADS):
                s1 = st_ref[2 * h + 1, :, cs]
                e1 = e_ref[2 * h + 1, :, cs]
                tau = tau_ref[h:h + 1, cs]
                for k in range(group):
                    s0 = st_ref[2 * h, pl.ds(tiles[k], 1), :][:, cs]
                    e0 = e_ref[2 * h, pl.ds(tiles[k], 1), 0:tn][:, cs]
                    term = jnp.where(s1 + s0 >= tau, e1 * e0, 0.0)
                    gates[k] = term if gates[k] is None else gates[k] + term
            for k in range(group):
                arow = slice((e0i + k) * nk, (e0i + k + 1) * nk)
                ws[e0i + k][c] = (gates[k] * act[arow, cs]).astype(BF16)
    w_all = jnp.concatenate([jnp.concatenate(row, 1) for row in ws], 0)
    acc_ref[...] += _dot(w_all, v_ref[...], TN)

    @pl.when(ei == pl.num_programs(2) - 1)
    def _():
        y = ALPHA * x_ref[...] + g_ref[...] * acc_ref[...].reshape(bb, tt, d)
        y_ref[...] = _layer_norm(y, lg_ref[...], lb_ref[...])


def _peer(x, mod, layer, wq_hi, wq_lo, k3, u_bf, v_bf, ln_g, ln_b, bb, tt, eb):
    b, t, d = x.shape
    nt = t // tt
    tn = bb * tt
    nhp = 2 * PEER_HEADS
    dk = wq_hi.shape[-1] // nhp
    nk = k3.shape[2]
    xspec = pl.BlockSpec((bb, tt, d), lambda bi, ti, ei: (bi, ti, 0))
    vec = pl.BlockSpec((1, d), lambda bi, ti, ei: (0, 0))

    def mod_spec(j):
        return pl.BlockSpec((None, None, bb, 1, d), lambda bi, ti, ei: (layer, j, bi, 0, 0))

    ne = nk // eb
    const = dict(pipeline_mode=pl.Buffered(1))
    wq_spec = pl.BlockSpec((None, d, nhp * dk), lambda bi, ti, ei: (layer, 0, 0), **const)
    k_spec = pl.BlockSpec((None, nhp, nk, 3 * dk), lambda bi, ti, ei: (layer, 0, 0, 0), **const)
    tab_spec = pl.BlockSpec((None, eb * nk, d), lambda bi, ti, ei: (layer, ei, 0))
    return pl.pallas_call(
        _peer_kernel,
        grid=(b // bb, nt, ne),
        in_specs=[xspec, mod_spec(4), mod_spec(3), mod_spec(5), wq_spec, wq_spec, k_spec,
                  tab_spec, tab_spec, vec, vec],
        out_specs=xspec,
        out_shape=jax.ShapeDtypeStruct((b, t, d), F32),
        scratch_shapes=[pltpu.VMEM((tn, d), BF16), pltpu.VMEM((nhp, tn, 3 * dk), BF16),
                        pltpu.VMEM((nhp, nk, tn), F32), pltpu.VMEM((nhp, nk, tn + LANE), F32),
                        pltpu.VMEM((PEER_HEADS, tn), F32),
                        pltpu.VMEM((tn, d), F32)],
        compiler_params=_cparams("arbitrary", "arbitrary", "arbitrary"),
        name="peer",
    )(x, mod, mod, mod, wq_hi, wq_lo, k3, u_bf, v_bf, ln_g, ln_b)


def _block_diag(w):
    l, nb, bw, _ = w.shape
    eye = jnp.eye(nb, dtype=w.dtype)
    return jnp.einsum("lncd,nm->lncmd", w, eye).reshape(l, nb * bw, nb * bw)


def _pick_tile(t, target):
    tt = min(t, target)
    while t % tt:
        tt //= 2
    return tt


def kernel(x_prompt, x_sample, cache_k, cache_v, cache_idx_k, state_conv, state_lru, page_table,
           c_prompt, c_sample, w_ada, b_ada, w_in, conv_w, conv_b, lru_wa, lru_ba, lru_wx, lru_bx,
           lru_lambda, beta_lru, beta_attn, w_out, ln1_g, ln1_b, ln2_g, ln2_b, peer_wq, peer_keys,
           peer_u, peer_v):
    depth = w_ada.shape[0]
    bp, tp, d = x_prompt.shape
    bs, ts, _ = x_sample.shape
    lw = d // 2
    aw = N_HEADS * HEAD_DIM
    page = cache_k.shape[2]
    past = page_table.shape[1] * page
    assert tp >= CONV_W - 1 and ts >= CONV_W - 1 and ts % 8 == 0 and bs % 8 == 0
    assert cache_k.shape[3:] == (N_HEADS, HEAD_DIM) and page >= ts and HEAD_DIM == 64

    in_cols = w_in.shape[-1]
    cols_pad = -(-in_cols // LANE) * LANE
    w_in_hi, w_in_lo = _split(jnp.pad(w_in, ((0, 0), (0, 0), (0, cols_pad - in_cols))))
    w_out_hi, w_out_lo = _split(w_out)
    wa_hi, wa_lo = _split(_block_diag(lru_wa))
    wx_hi, wx_lo = _split(_block_diag(lru_wx))
    nhp = 2 * PEER_HEADS
    dk = peer_wq.shape[-1] // nhp
    wq_hi, wq_lo = _split(peer_wq)
    nk = peer_keys.shape[3]
    pk = peer_keys.transpose(0, 2, 1, 3, 4).reshape(depth, nhp, nk, dk)
    pk_hi, pk_lo = _split(pk)
    pk3 = jnp.concatenate([pk_hi, pk_hi, pk_lo], -1)
    u_bf = peer_u.astype(BF16)
    v_bf = peer_v.astype(BF16)
    eb = 4 if nk % 4 == 0 else 1
    pool_k = cache_k.transpose(0, 1, 3, 4, 2)
    pool_v = cache_v.transpose(0, 1, 3, 4, 2)
    pool_ki = cache_idx_k.transpose(0, 1, 3, 2)

    nb_all = bp + bs
    nb_pad = -(-nb_all // 8) * 8
    c_all = jnp.pad(jnp.concatenate([c_prompt, c_sample], 0), ((0, nb_pad - nb_all), (0, 0)))
    mod = _ada_mod(c_all, w_ada, b_ada)
    mod_p = mod[:, :, :bp].reshape(depth, 6, bp, 1, d)
    mod_s = mod[:, :, bp:nb_all].reshape(depth, 6, bs, 1, d)

    tt_p = _pick_tile(tp, 512)
    tabs_p = [a.reshape(tp // tt_p, tt_p, LANE) for a in _rope_tables(jnp.arange(tp, dtype=I32))]
    bb_s = _pick_tile(bs, 32)
    tabs_s = [jnp.tile(a, (bb_s, 1)).reshape(1, bb_s * ts, LANE)
              for a in _rope_tables(past + jnp.arange(ts, dtype=I32))]

    tt_lru = _pick_tile(tp, 256)
    tq = _pick_tile(tp, Q_BLOCK)
    n_ranges = max(r for r in (1, 2, 4, 8, 16) if (tp // tq) % r == 0)
    tn_p = _pick_tile(tp, 512)
    bb_peer = _pick_tile(bs, max(1, 512 // ts))

    yp, ys = x_prompt, x_sample
    outs = {k: [] for k in ("kp", "vp", "kip", "cp", "hp", "ks", "vs", "kis", "cs", "hs")}
    for l in range(depth):
        lru_w = (conv_w[l], conv_b[l][None], wa_hi[l], wa_lo[l], lru_ba[l][None], wx_hi[l], wx_lo[l],
                 lru_bx[l][None], lru_lambda[l][None], beta_lru[l][None])
        zx, zg, q, k, v, qi, tail, ki, ki3, k16, v16 = _in_proj(yp, mod_p, l, w_in_hi, w_in_lo,
                                                                tabs_p, 1, tt_p)
        zc = jnp.zeros((bp, CONV_W - 1, lw), F32)
        zh = jnp.zeros((bp, lw), F32)
        mix_lru, nconv, nh = _lru_prompt(zx, zg, zc, zh, lru_w, bp, tp, tt_lru)
        o = _attn_prompt(qi, tail, ki3, q, k16, v16, bp, tp, tq, n_ranges)
        x1 = _out_proj(yp, mod_p, l, mix_lru, o, beta_attn[l][None], w_out_hi, w_out_lo,
                       ln1_g[l][None], ln1_b[l][None], 1, tt_p)
        yp = _peer(x1, mod_p, l, wq_hi, wq_lo, pk3, u_bf, v_bf,
                   ln2_g[l][None], ln2_b[l][None], 1, tn_p, eb)
        outs["kp"].append(k.reshape(bp, tp, N_HEADS, HEAD_DIM))
        outs["vp"].append(v.reshape(bp, tp, N_HEADS, HEAD_DIM))
        outs["kip"].append(ki.reshape(bp, tp, IDX_DIM))
        outs["cp"].append(nconv)
        outs["hp"].append(nh.reshape(bp, lw))
        zx, zg, q, k, v, qi, tail, ki = _in_proj(ys, mod_s, l, w_in_hi, w_in_lo, tabs_s, bb_s, ts)[:8]
        tm = lambda a: a.reshape(bs, ts, lw).transpose(1, 0, 2)
        mix_t, nconv_t, nh = _lru_sample(tm(zx), tm(zg), state_conv[l].transpose(1, 0, 2),
                                         state_lru[l], lru_w)
        mix_lru = mix_t.transpose(1, 0, 2).reshape(bs * ts, lw)
        o = _attn_sample(page_table, qi, tail, q, k, v, pool_ki, pool_k, pool_v, l, bs, ts)
        x1 = _out_proj(ys, mod_s, l, mix_lru, o, beta_attn[l][None], w_out_hi, w_out_lo,
                       ln1_g[l][None], ln1_b[l][None], bb_s, ts)
        ys = _peer(x1, mod_s, l, wq_hi, wq_lo, pk3, u_bf, v_bf,
                   ln2_g[l][None], ln2_b[l][None], bb_peer, ts, eb)
        outs["ks"].append(k.reshape(bs, ts, N_HEADS, HEAD_DIM))
        outs["vs"].append(v.reshape(bs, ts, N_HEADS, HEAD_DIM))
        outs["kis"].append(ki.reshape(bs, ts, IDX_DIM))
        outs["cs"].append(nconv_t.transpose(1, 0, 2))
        outs["hs"].append(nh)

    st = lambda name: jnp.stack(outs[name])
    return (yp, ys, st("kp"), st("vp"), st("kip"), st("cp"), st("hp"),
            st("ks"), st("vs"), st("kis"), st("cs"), st("hs"))
```

```python
import functools
import math

import jax
import jax.numpy as jnp
from jax import lax
from jax.experimental import pallas as pl
from jax.experimental.pallas import tpu as pltpu

F32 = jnp.float32
BF16 = jnp.bfloat16
I32 = jnp.int32

LRU_C = 8.0
CONV_W = 4
N_HEADS = 8
HEAD_DIM = 64
IDX_HEADS = 4
IDX_DIM = 64
TOPK_MAX = 256
ROPE_THETA = 500000.0
ROPE_FRAC_DIV = 4
Q_BLOCK = 128
PEER_HEADS = 8
PEER_TOPK = 16
DEPTH_NOMINAL = 4
ALPHA = (2 * DEPTH_NOMINAL) ** 0.25
LN_EPS = 1e-5

LANE = 128
VMEM_LIMIT = 56 * 1024 * 1024

NN = (((1,), (0,)), ((), ()))
NT = (((1,), (1,)), ((), ()))
TN = (((0,), (0,)), ((), ()))

NEG_INF = float("-inf")
INT_MIN = -2147483648
KEY_NEG_INF = -2139095041


def _cparams(*sem):
    return pltpu.CompilerParams(dimension_semantics=sem, vmem_limit_bytes=VMEM_LIMIT)


def _split(a):
    hi = a.astype(BF16)
    lo = (a - hi.astype(F32)).astype(BF16)
    return hi, lo


def _dot(a, b, dims=NN):
    return lax.dot_general(a, b, dims, preferred_element_type=F32)


def _dot3s(ah, al, bh, bl, dims=NN):
    return _dot(ah, bh, dims) + (_dot(ah, bl, dims) + _dot(al, bh, dims))


def _dot3w(a, bh, bl, dims=NN):
    ah, al = _split(a)
    return _dot3s(ah, al, bh, bl, dims)


def _cat3(x, lhs):
    hi = x.astype(BF16).astype(F32)
    lo = (x - hi).astype(BF16).astype(F32)
    parts = [hi, hi, lo] if lhs else [hi, lo, hi]
    return jnp.concatenate(parts, 1).astype(BF16)


def _gelu(x):
    return 0.5 * x * (1.0 + lax.erf(x * (1.0 / math.sqrt(2.0))))


def _layer_norm(y, g, b):
    mu = jnp.mean(y, -1, keepdims=True)
    d = y - mu
    var = jnp.mean(d * d, -1, keepdims=True)
    return d * lax.rsqrt(var + LN_EPS) * g + b


def _rms(y, g):
    return y * lax.rsqrt(jnp.mean(y * y, -1, keepdims=True) + LN_EPS) * g


def _order_key(x):
    b = pltpu.bitcast(x, I32)
    return jnp.where(b < 0, b ^ 0x7FFFFFFF, b)


def _ada_kernel(c_ref, w_ref, b_ref, o_ref):
    c = c_ref[...]
    s = c * jax.nn.sigmoid(c)
    wh, wl = _split(w_ref[...])
    o_ref[...] = _dot3w(s, wh, wl) + b_ref[...]


def _ada_mod(c_all, w_ada, b_ada):
    depth, d, six_d = w_ada.shape
    nb = c_all.shape[0]
    return pl.pallas_call(
        _ada_kernel,
        grid=(depth, six_d // d),
        in_specs=[
            pl.BlockSpec((nb, d), lambda l, j: (0, 0)),
            pl.BlockSpec((None, d, d), lambda l, j: (l, 0, j)),
            pl.BlockSpec((None, None, 1, d), lambda l, j: (l, j, 0, 0)),
        ],
        out_specs=pl.BlockSpec((None, None, nb, d), lambda l, j: (l, j, 0, 0)),
        out_shape=jax.ShapeDtypeStruct((depth, six_d // d, nb, d), F32),
        compiler_params=_cparams("arbitrary", "arbitrary"),
        name="ada_mod",
    )(c_all, w_ada, b_ada.reshape(depth, six_d // d, 1, d))


def _rope(x, c, s1, s2):
    w = x.shape[-1]
    return x * c + pltpu.roll(x, 8, 1) * s1 + pltpu.roll(x, w - 8, 1) * s2


def _inproj_kernel(x_ref, sc_ref, sh_ref, wh_ref, wl_ref, rc_ref, rs1_ref, rs2_ref,
                   tc_ref, ts1_ref, ts2_ref,
                   zx_ref, zg_ref, q_ref, k_ref, v_ref, qi_ref, tail_ref, ki_ref, ki3_ref,
                   kb_ref, vb_ref):
    bb, tt, d = x_ref.shape
    h = x_ref[...] * (1.0 + sc_ref[...]) + sh_ref[...]
    hh, hl = _split(h.reshape(bb * tt, d))

    def proj(lo, hi):
        return _dot3s(hh, hl, wh_ref[:, lo:hi], wl_ref[:, lo:hi])

    def proj1(lo, hi):
        return _dot(hh, wh_ref[:, lo:hi])

    lw = zx_ref.shape[-1]
    aw = q_ref.shape[-1]
    iw = qi_ref.shape[-1]
    o = 0
    zx_ref[...] = proj1(o, o + lw); o += lw
    zg_ref[...] = proj1(o, o + lw); o += lw
    rc, rs1, rs2 = rc_ref[...], rs1_ref[...], rs2_ref[...]
    rep = aw // LANE
    c4 = jnp.concatenate([rc] * rep, 1)
    s14 = jnp.concatenate([rs1] * rep, 1)
    s24 = jnp.concatenate([rs2] * rep, 1)
    q_ref[...] = _rope(proj1(o, o + aw), c4, s14, s24); o += aw
    kk = _rope(proj1(o, o + aw), c4, s14, s24); o += aw
    k_ref[...] = kk
    kb_ref[...] = kk.astype(BF16)
    vv = proj1(o, o + aw); o += aw
    v_ref[...] = vv
    vb_ref[...] = vv.astype(BF16)
    repi = iw // LANE
    qi_ref[...] = _rope(proj(o, o + iw), jnp.concatenate([rc] * repi, 1),
                        jnp.concatenate([rs1] * repi, 1), jnp.concatenate([rs2] * repi, 1)); o += iw
    tail = _rope(proj(o, o + LANE), tc_ref[...], ts1_ref[...], ts2_ref[...])
    tail_ref[...] = tail
    ki_ref[...] = tail[:, :IDX_DIM]
    ki3_ref[...] = _cat3(tail[:, :IDX_DIM], lhs=False)


def _rope_tables(pos):
    t = pos.shape[0]
    rot = HEAD_DIM // ROPE_FRAC_DIV
    half = rot // 2
    freqs = ROPE_THETA ** (-jnp.arange(half, dtype=F32) / half)
    ang = pos.astype(F32)[:, None] * freqs[None, :]
    cos, sin = jnp.cos(ang), jnp.sin(ang)
    ones = jnp.ones((t, HEAD_DIM - rot), F32)
    zeros = jnp.zeros((t, HEAD_DIM - rot), F32)
    zh = jnp.zeros((t, half), F32)
    c64 = jnp.concatenate([cos, cos, ones], 1)
    s1_64 = jnp.concatenate([zh, sin, zeros], 1)
    s2_64 = jnp.concatenate([-sin, zh, zeros], 1)
    rc = jnp.concatenate([c64, c64], 1)
    rs1 = jnp.concatenate([s1_64, s1_64], 1)
    rs2 = jnp.concatenate([s2_64, s2_64], 1)
    wscale = jnp.concatenate([jnp.full((t, IDX_HEADS), IDX_HEADS ** -0.5, F32),
                              jnp.ones((t, LANE - IDX_DIM - IDX_HEADS), F32)], 1)
    z64 = jnp.zeros((t, LANE - IDX_DIM), F32)
    tc = jnp.concatenate([c64, wscale], 1)
    ts1 = jnp.concatenate([s1_64, z64], 1)
    ts2 = jnp.concatenate([s2_64, z64], 1)
    return rc, rs1, rs2, tc, ts1, ts2


def _in_proj(x, mod, layer, w_hi, w_lo, tables, bb, tt):
    b, t, d = x.shape
    n = b * t
    rows = bb * tt
    nt = t // tt
    lw = d // 2
    aw = N_HEADS * HEAD_DIM
    iw = IDX_HEADS * IDX_DIM
    cols = w_hi.shape[-1]
    tab_idx = (lambda bi, ti: (ti, 0, 0)) if tables[0].shape[0] > 1 else (lambda bi, ti: (0, 0, 0))
    tab_spec = pl.BlockSpec((None, rows, LANE), tab_idx)
    row_idx = lambda bi, ti: (bi * nt + ti, 0)

    def mod_spec(j):
        return pl.BlockSpec((None, None, bb, 1, d), lambda bi, ti: (layer, j, bi, 0, 0))

    outs = [(n, lw), (n, lw), (n, aw), (n, aw), (n, aw), (n, iw), (n, LANE), (n, IDX_DIM),
            (n, 3 * IDX_DIM), (n, aw), (n, aw)]
    dtypes = [F32] * 8 + [BF16] * 3
    return pl.pallas_call(
        _inproj_kernel,
        grid=(b // bb, nt),
        in_specs=[
            pl.BlockSpec((bb, tt, d), lambda bi, ti: (bi, ti, 0)),
            mod_spec(1), mod_spec(0),
            pl.BlockSpec((None, d, cols), lambda bi, ti: (layer, 0, 0)),
            pl.BlockSpec((None, d, cols), lambda bi, ti: (layer, 0, 0)),
        ] + [tab_spec] * 6,
        out_specs=[pl.BlockSpec((rows, w), row_idx) for _, w in outs],
        out_shape=[jax.ShapeDtypeStruct(s, dt) for s, dt in zip(outs, dtypes)],
        compiler_params=_cparams("arbitrary", "arbitrary"),
        name="in_proj",
    )(x, mod, mod, w_hi, w_lo, *tables)


def _lru_gates(xc, wah, wal, ba, wxh, wxl, bx, lam):
    xh, xl = _split(xc)
    r = jax.nn.sigmoid(_dot3s(xh, xl, wah, wal) + ba)
    i = jax.nn.sigmoid(_dot3s(xh, xl, wxh, wxl) + bx)
    nl = -lam
    sp = jnp.maximum(nl, 0.0) + jnp.log1p(jnp.exp(-jnp.abs(nl)))
    log_a = -LRU_C * r * sp
    a = jnp.exp(log_a)
    u = jnp.sqrt(1.0 - a * a) * (i * xc)
    return a, u


def _lru_prompt_kernel(zx_ref, zg_ref, cbuf_ref, h0_ref, cw_ref, cb_ref, wah_ref, wal_ref, ba_ref,
                       wxh_ref, wxl_ref, bx_ref, lam_ref, beta_ref,
                       mix_ref, nconv_ref, nh_ref, ext_ref, hst_ref):
    tt = zx_ref.shape[0]
    ti = pl.program_id(1)

    @pl.when(ti == 0)
    def _():
        ext_ref[5:8, :] = cbuf_ref[...]
        hst_ref[0:1, :] = h0_ref[...]

    x = zx_ref[...]
    ext_ref[8:8 + tt, :] = x
    w = cw_ref[...]
    xc = (cb_ref[...] + ext_ref[5:5 + tt, :] * w[0:1] + ext_ref[6:6 + tt, :] * w[1:2]
          + ext_ref[7:7 + tt, :] * w[2:3] + x * w[3:4])
    tail3 = ext_ref[tt + 5:tt + 8, :]
    ext_ref[5:8, :] = tail3

    a, u = _lru_gates(xc, wah_ref[...], wal_ref[...], ba_ref[...], wxh_ref[...], wxl_ref[...],
                      bx_ref[...], lam_ref[...])
    row = lax.broadcasted_iota(I32, (tt, 1), 0)
    s = 1
    while s < tt:
        a_sh = pltpu.roll(a, s, 0)
        u_sh = pltpu.roll(u, s, 0)
        m = row >= s
        u = u + a * jnp.where(m, u_sh, 0.0)
        a = a * jnp.where(m, a_sh, 1.0)
        s *= 2
    h = a * hst_ref[0:1, :] + u
    hst_ref[0:1, :] = h[tt - 1:tt, :]
    y = _gelu(zg_ref[...]) * h
    mix_ref[...] = _rms(y, beta_ref[...])

    @pl.when(ti == pl.num_programs(1) - 1)
    def _():
        nconv_ref[...] = tail3
        nh_ref[...] = h[tt - 1:tt, :]


def _lru_prompt(zx, zg, cbuf, h0, lw, b, t, tt):
    n, w = zx.shape
    nt = t // tt
    row_spec = pl.BlockSpec((tt, w), lambda bi, ti: (bi * nt + ti, 0))
    full = lambda shape: pl.BlockSpec(shape, lambda bi, ti: tuple(0 for _ in shape))
    vec = full((1, w))
    mat = full((w, w))
    return pl.pallas_call(
        _lru_prompt_kernel,
        grid=(b, nt),
        in_specs=[row_spec, row_spec,
                  pl.BlockSpec((None, CONV_W - 1, w), lambda bi, ti: (bi, 0, 0)),
                  pl.BlockSpec((None, 1, w), lambda bi, ti: (bi, 0, 0)),
                  full((CONV_W, w)), vec, mat, mat, vec, mat, mat, vec, vec, vec],
        out_specs=[row_spec,
                   pl.BlockSpec((None, CONV_W - 1, w), lambda bi, ti: (bi, 0, 0)),
                   pl.BlockSpec((None, 1, w), lambda bi, ti: (bi, 0, 0))],
        out_shape=[jax.ShapeDtypeStruct((n, w), F32),
                   jax.ShapeDtypeStruct((b, CONV_W - 1, w), F32),
                   jax.ShapeDtypeStruct((b, 1, w), F32)],
        scratch_shapes=[pltpu.VMEM((tt + 8, w), F32), pltpu.VMEM((8, w), F32)],
        compiler_params=_cparams("arbitrary", "arbitrary"),
        name="lru_prompt",
    )(zx, zg, cbuf, h0.reshape(b, 1, w), *lw)


def _lru_sample_kernel(zx_ref, zg_ref, cbuf_ref, h0_ref, cw_ref, cb_ref, wah_ref, wal_ref, ba_ref,
                       wxh_ref, wxl_ref, bx_ref, lam_ref, beta_ref,
                       mix_ref, nconv_ref, nh_ref, xc_ref):
    t, b, wd = zx_ref.shape
    w = cw_ref[...]
    xp = [cbuf_ref[k] for k in range(CONV_W - 1)] + [zx_ref[k] for k in range(t)]
    for k in range(t):
        xc_ref[k * b:(k + 1) * b, :] = (cb_ref[...] + xp[k] * w[0:1] + xp[k + 1] * w[1:2]
                                        + xp[k + 2] * w[2:3] + xp[k + 3] * w[3:4])
    for k in range(CONV_W - 1):
        nconv_ref[k] = xp[t + k]
    a, u = _lru_gates(xc_ref[...], wah_ref[...], wal_ref[...], ba_ref[...], wxh_ref[...],
                      wxl_ref[...], bx_ref[...], lam_ref[...])
    h = h0_ref[...]
    beta = beta_ref[...]
    for k in range(t):
        h = a[k * b:(k + 1) * b, :] * h + u[k * b:(k + 1) * b, :]
        mix_ref[k] = _rms(_gelu(zg_ref[k]) * h, beta)
    nh_ref[...] = h


def _lru_sample(zx_t, zg_t, cbuf_t, h0, lw):
    t, b, w = zx_t.shape
    full = lambda shape: pl.BlockSpec(shape, lambda i: tuple(0 for _ in shape))
    vec = full((1, w))
    mat = full((w, w))
    return pl.pallas_call(
        _lru_sample_kernel,
        grid=(1,),
        in_specs=[full((t, b, w)), full((t, b, w)), full((CONV_W - 1, b, w)), full((b, w)),
                  full((CONV_W, w)), vec, mat, mat, vec, mat, mat, vec, vec, vec],
        out_specs=[full((t, b, w)), full((CONV_W - 1, b, w)), full((b, w))],
        out_shape=[jax.ShapeDtypeStruct((t, b, w), F32),
                   jax.ShapeDtypeStruct((CONV_W - 1, b, w), F32),
                   jax.ShapeDtypeStruct((b, w), F32)],
        scratch_shapes=[pltpu.VMEM((t * b, w), F32)],
        compiler_params=_cparams("arbitrary"),
        name="lru_sample",
    )(zx_t, zg_t, cbuf_t, h0, *lw)


def _select_bias(score, qpos, key_ref, topk):
    q, s = score.shape
    kpos = lax.broadcasted_iota(I32, (q, s), 1)
    vis = kpos <= qpos
    key_ref[...] = jnp.where(vis, _order_key(score + 0.0), KEY_NEG_INF)

    kf = float(topk)

    ngrp = 2 if q % (2 * SUBLANES) == 0 else 1
    rg = q // ngrp
    unroll = 4

    def count_ge(g, t):
        keys = key_ref[g * rg:(g + 1) * rg, :]
        return jnp.sum(jnp.where(keys >= t, 1.0, 0.0), axis=1, keepdims=True)

    init = []
    for g in range(ngrp):
        c0 = count_ge(g, jnp.zeros((rg, 1), I32))
        small = (qpos[g * rg:(g + 1) * rg] + 1) <= topk
        init.append(jnp.where(small, KEY_NEG_INF + 1, jnp.where(c0 >= kf, 0, INT_MIN)).astype(I32))
        init.append(jnp.where(small, 1, jnp.where(c0 == kf, 1, 0)).astype(I32))

    def unresolved(state):
        m = state[1]
        for g in range(1, ngrp):
            m = jnp.minimum(m, state[2 * g + 1])
        return jnp.min(m) == 0

    def vcond(c):
        return jnp.logical_and(c[0] < 31, unresolved(c[1:]))

    def vbody(c):
        i, state = c[0], list(c[1:])
        for u in range(unroll):
            bit = jnp.where(i + u < 31, jnp.left_shift(jnp.int32(1), jnp.maximum(30 - i - u, 0)), 0)
            for g in range(ngrp):
                t, res = state[2 * g], state[2 * g + 1]
                cand = t | bit
                cnt = count_ge(g, cand)
                state[2 * g] = jnp.where(res > 0, t, jnp.where(cnt >= kf, cand, t))
                state[2 * g + 1] = jnp.where(cnt == kf, 1, res)
        return (i + unroll, *state)

    if q > SUBLANES:
        out = lax.while_loop(vcond, vbody, (jnp.int32(0), *init))
        thr = jnp.concatenate([out[1 + 2 * g] for g in range(ngrp)], 0)
        res = jnp.concatenate([out[2 + 2 * g] for g in range(ngrp)], 0)
    else:
        keys = key_ref[...]
        thr, res = init
        hi = 31
        while hi > 0:
            nb = min(RADIX_BITS, hi)
            shift = hi - nb
            best_t, best_c = thr, jnp.full((q, 1), -1.0, F32)
            for j in range(1, 1 << nb):
                cand = thr | (j << shift)
                cnt = jnp.sum(jnp.where(keys >= cand, 1.0, 0.0), axis=1, keepdims=True)
                ok = cnt >= kf
                best_t = jnp.where(ok, cand, best_t)
                best_c = jnp.where(ok, cnt, best_c)
            thr = jnp.where(res > 0, thr, best_t)
            res = jnp.where(best_c == kf, 1, res)
            hi = shift
    nbits = max(1, (s - 1).bit_length())

    def tie_phase():
        c_gt = jnp.sum(jnp.where(key_ref[...] > thr, 1.0, 0.0), axis=1, keepdims=True)
        need = kf - c_gt

        def ibody(i, j):
            cand = j + jnp.left_shift(jnp.int32(1), nbits - 1 - i)
            kp = lax.broadcasted_iota(I32, (q, s), 1)
            hit = jnp.where(key_ref[...] == thr, jnp.where(kp <= cand, 1.0, 0.0), 0.0)
            f = jnp.sum(hit, axis=1, keepdims=True)
            return jnp.where(f < need, cand, j)

        return lax.fori_loop(0, nbits, ibody, jnp.full((q, 1), -1, I32)) + 1

    jsel = lax.cond(jnp.min(res) == 0, tie_phase, lambda: jnp.full((q, 1), s, I32))
    jsel = jnp.where(res > 0, s, jsel)
    key = key_ref[...]
    sel = jnp.where(key > thr, 1, jnp.where(key == thr, jnp.where(kpos <= jsel, 1, 0), 0))
    sel = jnp.where(vis, sel, 0)
    return jnp.where(sel > 0, 0.0, NEG_INF)


def _attn_prompt_kernel(qi_ref, tailq_ref, ki3_ref, q_ref, k_ref, v_ref, o_ref, bias_ref, key_ref,
                        *, topk, qb0):
    tq = q_ref.shape[0]
    ki3 = ki3_ref[...]
    tqv = tailq_ref[...]
    qi = qi_ref[...]
    score = None
    for h in range(IDX_HEADS):
        lg = _dot(_cat3(qi[:, h * IDX_DIM:(h + 1) * IDX_DIM], lhs=True), ki3, NT)
        term = tqv[:, IDX_DIM + h:IDX_DIM + h + 1] * jnp.maximum(lg, 0.0)
        score = term if score is None else score + term
    qpos = (qb0 + pl.program_id(1)) * tq + lax.broadcasted_iota(I32, (tq, 1), 0)
    bias_ref[0:tq, :] = _select_bias(score, qpos, key_ref, topk)
    bias_ref[tq:2 * tq, :] = bias_ref[0:tq, :]

    low = lax.broadcasted_iota(I32, (tq, LANE), 1) < HEAD_DIM
    for g in range(q_ref.shape[1] // LANE):
        cs = slice(g * LANE, (g + 1) * LANE)
        qg = q_ref[:, cs] * (HEAD_DIM ** -0.5)
        qpair = jnp.concatenate([jnp.where(low, qg, 0.0), jnp.where(low, 0.0, qg)], 0).astype(BF16)
        s = _dot(qpair, k_ref[:, cs], NT) + bias_ref[...]
        m = jnp.max(s, axis=1, keepdims=True)
        p = jnp.exp(s - m)
        l = jnp.sum(p, axis=1, keepdims=True)
        og = _dot(p.astype(BF16), v_ref[:, cs]) / l
        o_ref[:, cs] = jnp.where(low, og[0:tq], og[tq:2 * tq])


def _attn_prompt(qi, tail, ki3, q, k_bf, v_bf, b, t, tq, n_ranges):
    nq = t // tq
    topk = min(TOPK_MAX, t // 4)
    iw = qi.shape[-1]
    aw = q.shape[-1]
    nqr = nq // n_ranges
    keys3 = lambda a: a.reshape(b, t, a.shape[-1])
    outs = []
    for r in range(n_ranges):
        qb0 = r * nqr
        tv = (r + 1) * nqr * tq
        qrow = lambda w, qb0=qb0: pl.BlockSpec((tq, w), lambda bi, qb: (bi * nq + qb0 + qb, 0))
        kvis = lambda w, tv=tv: pl.BlockSpec((None, tv, w), lambda bi, qb: (bi, 0, 0))
        outs.append(pl.pallas_call(
            functools.partial(_attn_prompt_kernel, topk=topk, qb0=qb0),
            grid=(b, nqr),
            in_specs=[qrow(iw), qrow(LANE), kvis(3 * IDX_DIM), qrow(aw), kvis(aw), kvis(aw)],
            out_specs=pl.BlockSpec((None, tq, aw), lambda bi, qb: (bi, qb, 0)),
            out_shape=jax.ShapeDtypeStruct((b, nqr * tq, aw), F32),
            scratch_shapes=[pltpu.VMEM((2 * tq, tv), F32), pltpu.VMEM((tq, tv), I32)],
            compiler_params=_cparams("arbitrary", "arbitrary"),
            name="attn_prompt",
        )(qi, tail, keys3(ki3), q, keys3(k_bf), keys3(v_bf)))
    return jnp.concatenate(outs, 1).reshape(b * t, aw)


def _attn_sample_kernel(pt_ref, qi_ref, tail_ref, q_ref, kn_ref, vn_ref, *rest, n_pages, page, topk):
    ki_pages = rest[:n_pages]
    k_pages = rest[n_pages:2 * n_pages]
    v_pages = rest[2 * n_pages:3 * n_pages]
    o_ref, kit_ref, kst_ref, vst_ref, key_ref = rest[3 * n_pages:]
    t = q_ref.shape[0]
    past = n_pages * page
    hsl = lambda h: slice(h * HEAD_DIM, (h + 1) * HEAD_DIM)
    for j in range(n_pages):
        cols = slice(j * page, (j + 1) * page)
        kit = ki_pages[j][...]
        hi = kit.astype(BF16)
        lo = (kit - hi.astype(F32)).astype(BF16)
        kit_ref[0:IDX_DIM, cols] = hi
        kit_ref[IDX_DIM:2 * IDX_DIM, cols] = lo
        kit_ref[2 * IDX_DIM:3 * IDX_DIM, cols] = hi
        kst_ref[:, :, cols] = k_pages[j][...].astype(BF16)
        vst_ref[:, :, cols] = v_pages[j][...].astype(BF16)
    tl = tail_ref[...]
    ki_new = _cat3(jnp.concatenate([tl[:, :IDX_DIM], jnp.zeros((page - t, IDX_DIM), F32)], 0),
                   lhs=False)
    kn = kn_ref[...]
    vn = vn_ref[...]
    zpad = jnp.zeros((page - t, HEAD_DIM), F32)

    qi = qi_ref[...]
    qi_rows = jnp.concatenate([qi[:, h * IDX_DIM:(h + 1) * IDX_DIM] for h in range(IDX_HEADS)], 0)
    w_rows = jnp.concatenate([tl[:, IDX_DIM + h:IDX_DIM + h + 1] for h in range(IDX_HEADS)], 0)
    qi3 = _cat3(qi_rows, lhs=True)
    logits = jnp.concatenate([_dot(qi3, kit_ref[...]), _dot(qi3, ki_new, NT)], 1)
    sc = jnp.maximum(logits, 0.0) * w_rows
    score = sc[0:t]
    for h in range(1, IDX_HEADS):
        score = score + sc[h * t:(h + 1) * t]
    qpos = past + lax.broadcasted_iota(I32, (t, 1), 0)
    bias = _select_bias(score, qpos, key_ref, topk)

    q = q_ref[...] * (HEAD_DIM ** -0.5)
    rpad = -t % 16
    outs = []
    for h in range(N_HEADS):
        qh = jnp.concatenate([q[:, hsl(h)], jnp.zeros((rpad, HEAD_DIM), F32)], 0).astype(BF16)
        k_new = jnp.concatenate([kn[:, hsl(h)], zpad], 0).astype(BF16)
        v_new = jnp.concatenate([vn[:, hsl(h)], zpad], 0).astype(BF16)
        s = jnp.concatenate([_dot(qh, kst_ref[h]), _dot(qh, k_new, NT)], 1)[0:t] + bias
        m = jnp.max(s, axis=1, keepdims=True)
        p = jnp.exp(s - m)
        l = jnp.sum(p, axis=1, keepdims=True)
        ph = jnp.concatenate([p, jnp.zeros((rpad, past + page), F32)], 0).astype(BF16)
        o = _dot(ph[:, :past], vst_ref[h], NT) + _dot(ph[:, past:], v_new)
        outs.append(o[0:t] / l)
    o_ref[...] = jnp.concatenate(outs, 1)


def _attn_sample(page_table, qi, tail, q, kn, vn, pool_ki, pool_k, pool_v, layer, b, t):
    n_pages = page_table.shape[1]
    page = pool_ki.shape[3]
    past = n_pages * page
    topk = min(TOPK_MAX, (past + t) // 4)
    aw = q.shape[-1]
    iw = qi.shape[-1]
    s_pad = past + page
    row = lambda w: pl.BlockSpec((t, w), lambda bi, pt: (bi, 0))

    def ki_spec(j):
        return pl.BlockSpec((None, None, IDX_DIM, page), lambda bi, pt: (layer, pt[bi, j], 0, 0))

    def kv_spec(j):
        return pl.BlockSpec((None, None, N_HEADS, HEAD_DIM, page),
                            lambda bi, pt: (layer, pt[bi, j], 0, 0, 0))

    in_specs = [row(iw), row(LANE), row(aw), row(aw), row(aw)]
    in_specs += [ki_spec(j) for j in range(n_pages)]
    in_specs += [kv_spec(j) for j in range(n_pages)]
    in_specs += [kv_spec(j) for j in range(n_pages)]
    grid_spec = pltpu.PrefetchScalarGridSpec(
        num_scalar_prefetch=1,
        grid=(b,),
        in_specs=in_specs,
        out_specs=row(aw),
        scratch_shapes=[pltpu.VMEM((3 * IDX_DIM, past), BF16),
                        pltpu.VMEM((N_HEADS, HEAD_DIM, past), BF16),
                        pltpu.VMEM((N_HEADS, HEAD_DIM, past), BF16),
                        pltpu.VMEM((t, s_pad), I32)],
    )
    return pl.pallas_call(
        functools.partial(_attn_sample_kernel, n_pages=n_pages, page=page, topk=topk),
        grid_spec=grid_spec,
        out_shape=jax.ShapeDtypeStruct((b * t, aw), F32),
        compiler_params=_cparams("arbitrary"),
        name="attn_sample",
    )(page_table, qi, tail, q, kn, vn, *([pool_ki] * n_pages), *([pool_k] * n_pages),
      *([pool_v] * n_pages))


def _outproj_kernel(x_ref, g_ref, ml_ref, o_ref, beta_ref, wh_ref, wl_ref, lg_ref, lb_ref, y_ref):
    bb, tt, d = x_ref.shape
    lw = ml_ref.shape[-1]
    mo = _rms(o_ref[...], beta_ref[...])
    acc = (_dot3w(ml_ref[...], wh_ref[0:lw, :], wl_ref[0:lw, :])
           + _dot3w(mo, wh_ref[lw:, :], wl_ref[lw:, :]))
    y = ALPHA * x_ref[...] + g_ref[...] * acc.reshape(bb, tt, d)
    y_ref[...] = _layer_norm(y, lg_ref[...], lb_ref[...])


def _out_proj(x, mod, layer, mix_lru, o, beta_attn, w_hi, w_lo, ln_g, ln_b, bb, tt):
    b, t, d = x.shape
    nt = t // tt
    rows = bb * tt
    lw = mix_lru.shape[-1]
    aw = o.shape[-1]
    xspec = pl.BlockSpec((bb, tt, d), lambda bi, ti: (bi, ti, 0))
    vec = lambda w: pl.BlockSpec((1, w), lambda bi, ti: (0, 0))
    wspec = pl.BlockSpec((None, lw + aw, d), lambda bi, ti: (layer, 0, 0))
    return pl.pallas_call(
        _outproj_kernel,
        grid=(b // bb, nt),
        in_specs=[xspec,
                  pl.BlockSpec((None, None, bb, 1, d), lambda bi, ti: (layer, 2, bi, 0, 0)),
                  pl.BlockSpec((rows, lw), lambda bi, ti: (bi * nt + ti, 0)),
                  pl.BlockSpec((rows, aw), lambda bi, ti: (bi * nt + ti, 0)),
                  vec(aw), wspec, wspec, vec(d), vec(d)],
        out_specs=xspec,
        out_shape=jax.ShapeDtypeStruct((b, t, d), F32),
        compiler_params=_cparams("arbitrary", "arbitrary"),
        name="out_proj",
    )(x, mod, mix_lru, o, beta_attn, w_hi, w_lo, ln_g, ln_b)


SUBLANES = 8
RADIX_BITS = 4


def _bitonic_merge(seq):
    seq = list(seq)
    n = len(seq)
    d = n // 2
    while d >= 1:
        for i in range(n):
            if i & d == 0:
                a, b = seq[i], seq[i + d]
                seq[i], seq[i + d] = jnp.maximum(a, b), jnp.minimum(a, b)
        d //= 2
    return seq


def _sort_desc(seq):
    if len(seq) == 1:
        return list(seq)
    half = len(seq) // 2
    return _bitonic_merge(_sort_desc(seq[:half]) + _sort_desc(seq[half:])[::-1])


def _merge_top(a, b, cap):
    n = len(a)
    if 2 * n <= cap:
        return _bitonic_merge(a + b[::-1])
    return _bitonic_merge([jnp.maximum(a[k], b[n - 1 - k]) for k in range(n)])


def _merge_sublanes(s, cap):
    d = SUBLANES // 2
    while d >= 1:
        s = _merge_top(s, [pltpu.roll(v, SUBLANES - d, 0) for v in s], cap)
        d //= 2
    return s


def _top_sorted(x, cap):
    streams = [x[SUBLANES * k:SUBLANES * (k + 1)] for k in range(x.shape[0] // SUBLANES)]
    return _merge_sublanes(_sort_desc(streams)[:cap], cap)


def _peer_kernel(x_ref, sc_ref, sh_ref, g_ref, wqh_ref, wql_ref, k3_ref, u_ref, v_ref,
                 lg_ref, lb_ref, y_ref,
                 hb_ref, q3_ref, st_ref, e_ref, tau_ref, acc_ref):
    bb, tt, d = x_ref.shape
    tn = bb * tt
    nk = st_ref.shape[1]
    eb = u_ref.shape[0] // nk
    ei = pl.program_id(2)

    @pl.when(ei == 0)
    def _():
        h2 = (x_ref[...] * (1.0 + sc_ref[...]) + sh_ref[...]).reshape(tn, d)
        hh, hl = _split(h2)
        hb_ref[...] = hh
        acc_ref[...] = jnp.zeros_like(acc_ref)

        sub = lax.broadcasted_iota(I32, (SUBLANES, tn), 0)

        def pack(rows):
            out = jnp.broadcast_to(rows[0], (SUBLANES, tn))
            for k in range(1, SUBLANES):
                out = jnp.where(sub == k, jnp.broadcast_to(rows[k], (SUBLANES, tn)), out)
            return out

        q_all = _dot3s(hh, hl, wqh_ref[...], wql_ref[...])
        dk = q_all.shape[1] // (2 * PEER_HEADS)
        for hp in range(2 * PEER_HEADS):
            qh, ql = _split(q_all[:, hp * dk:(hp + 1) * dk])
            q3_ref[hp] = jnp.concatenate([qh, ql, qh], 1)

        def per_head(h, carry):
            sv = []
            for p in range(2):
                hp = 2 * h + p
                s_t = _dot(k3_ref[hp], q3_ref[hp], NT)
                st_ref[hp] = s_t
                sv.append([v[0:1] for v in _top_sorted(s_t, PEER_TOPK)])
            sv0, sv1 = sv
            m0, m1 = sv0[0], sv1[0]
            groups = []
            for g in range(PEER_TOPK // SUBLANES):
                base = pack(sv0[SUBLANES * g:SUBLANES * (g + 1)])
                groups.append([base + sv1[b] for b in range(PEER_TOPK)])
            top = groups[0]
            for g in range(1, len(groups)):
                top = _merge_top(top, groups[g], PEER_TOPK)
            top = [v[0:1] for v in _merge_sublanes(top, PEER_TOPK)]
            mtot = m0 + m1
            z = jnp.exp(top[0] - mtot)
            for r in range(1, PEER_TOPK):
                z = z + jnp.exp(top[r] - mtot)
            tau_ref[pl.ds(h, 1), :] = top[PEER_TOPK - 1]
            inv_z = 1.0 / z
            e_ref[2 * h, :, 0:tn] = jnp.exp(st_ref[2 * h] - m0)
            e_ref[2 * h + 1, :, 0:tn] = jnp.exp(st_ref[2 * h + 1] - m1) * inv_z
            return carry

        lax.fori_loop(0, PEER_HEADS, per_head, 0)

    act = _gelu(_dot(u_ref[...], hb_ref[...], NT))
    group = 2 if eb % 2 == 0 else 1
    cw = LANE if tn % LANE == 0 else tn
    ws = [[None] * (tn // cw) for _ in range(eb)]
    for e0i in range(0, eb, group):
        tiles = [ei * eb + e0i + k for k in range(group)]
        for c in range(tn // cw):
            cs = slice(c * cw, (c + 1) * cw)
            gates = [None] * group
            for h in range(PEER_HEADS):
                s1 = st_ref[2 * h + 1, :, cs]
                e1 = e_ref[2 * h + 1, :, cs]
                tau = tau_ref[h:h + 1, cs]
                for k in range(group):
                    s0 = st_ref[2 * h, pl.ds(tiles[k], 1), :][:, cs]
                    e0 = e_ref[2 * h, pl.ds(tiles[k], 1), 0:tn][:, cs]
                    term = jnp.where(s1 + s0 >= tau, e1 * e0, 0.0)
                    gates[k] = term if gates[k] is None else gates[k] + term
            for k in range(group):
                arow = slice((e0i + k) * nk, (e0i + k + 1) * nk)
                ws[e0i + k][c] = (gates[k] * act[arow, cs]).astype(BF16)
    w_all = jnp.concatenate([jnp.concatenate(row, 1) for row in ws], 0)
    acc_ref[...] += _dot(w_all, v_ref[...], TN)

    @pl.when(ei == pl.num_programs(2) - 1)
    def _():
        y = ALPHA * x_ref[...] + g_ref[...] * acc_ref[...].reshape(bb, tt, d)
        y_ref[...] = _layer_norm(y, lg_ref[...], lb_ref[...])


def _peer(x, mod, layer, wq_hi, wq_lo, k3, u_bf, v_bf, ln_g, ln_b, bb, tt, eb):
    b, t, d = x.shape
    nt = t // tt
    tn = bb * tt
    nhp = 2 * PEER_HEADS
    dk = wq_hi.shape[-1] // nhp
    nk = k3.shape[2]
    xspec = pl.BlockSpec((bb, tt, d), lambda bi, ti, ei: (bi, ti, 0))
    vec = pl.BlockSpec((1, d), lambda bi, ti, ei: (0, 0))

    def mod_spec(j):
        return pl.BlockSpec((None, None, bb, 1, d), lambda bi, ti, ei: (layer, j, bi, 0, 0))

    ne = nk // eb
    const = dict(pipeline_mode=pl.Buffered(1))
    wq_spec = pl.BlockSpec((None, d, nhp * dk), lambda bi, ti, ei: (layer, 0, 0), **const)
    k_spec = pl.BlockSpec((None, nhp, nk, 3 * dk), lambda bi, ti, ei: (layer, 0, 0, 0), **const)
    tab_spec = pl.BlockSpec((None, eb * nk, d), lambda bi, ti, ei: (layer, ei, 0))
    return pl.pallas_call(
        _peer_kernel,
        grid=(b // bb, nt, ne),
        in_specs=[xspec, mod_spec(4), mod_spec(3), mod_spec(5), wq_spec, wq_spec, k_spec,
                  tab_spec, tab_spec, vec, vec],
        out_specs=xspec,
        out_shape=jax.ShapeDtypeStruct((b, t, d), F32),
        scratch_shapes=[pltpu.VMEM((tn, d), BF16), pltpu.VMEM((nhp, tn, 3 * dk), BF16),
                        pltpu.VMEM((nhp, nk, tn), F32), pltpu.VMEM((nhp, nk, tn + LANE), F32),
                        pltpu.VMEM((PEER_HEADS, tn), F32),
                        pltpu.VMEM((tn, d), F32)],
        compiler_params=_cparams("arbitrary", "arbitrary", "arbitrary"),
        name="peer",
    )(x, mod, mod, mod, wq_hi, wq_lo, k3, u_bf, v_bf, ln_g, ln_b)


def _block_diag(w):
    l, nb, bw, _ = w.shape
    eye = jnp.eye(nb, dtype=w.dtype)
    return jnp.einsum("lncd,nm->lncmd", w, eye).reshape(l, nb * bw, nb * bw)


def _pick_tile(t, target):
    tt = min(t, target)
    while t % tt:
        tt //= 2
    return tt


def kernel(x_prompt, x_sample, cache_k, cache_v, cache_idx_k, state_conv, state_lru, page_table,
           c_prompt, c_sample, w_ada, b_ada, w_in, conv_w, conv_b, lru_wa, lru_ba, lru_wx, lru_bx,
           lru_lambda, beta_lru, beta_attn, w_out, ln1_g, ln1_b, ln2_g, ln2_b, peer_wq, peer_keys,
           peer_u, peer_v):
    depth = w_ada.shape[0]
    bp, tp, d = x_prompt.shape
    bs, ts, _ = x_sample.shape
    lw = d // 2
    aw = N_HEADS * HEAD_DIM
    page = cache_k.shape[2]
    past = page_table.shape[1] * page
    assert tp >= CONV_W - 1 and ts >= CONV_W - 1 and ts % 8 == 0 and bs % 8 == 0
    assert cache_k.shape[3:] == (N_HEADS, HEAD_DIM) and page >= ts and HEAD_DIM == 64

    in_cols = w_in.shape[-1]
    cols_pad = -(-in_cols // LANE) * LANE
    w_in_hi, w_in_lo = _split(jnp.pad(w_in, ((0, 0), (0, 0), (0, cols_pad - in_cols))))
    w_out_hi, w_out_lo = _split(w_out)
    wa_hi, wa_lo = _split(_block_diag(lru_wa))
    wx_hi, wx_lo = _split(_block_diag(lru_wx))
    nhp = 2 * PEER_HEADS
    dk = peer_wq.shape[-1] // nhp
    wq_hi, wq_lo = _split(peer_wq)
    nk = peer_keys.shape[3]
    pk = peer_keys.transpose(0, 2, 1, 3, 4).reshape(depth, nhp, nk, dk)
    pk_hi, pk_lo = _split(pk)
    pk3 = jnp.concatenate([pk_hi, pk_hi, pk_lo], -1)
    u_bf = peer_u.astype(BF16)
    v_bf = peer_v.astype(BF16)
    eb = 4 if nk % 4 == 0 else 1
    pool_k = cache_k.transpose(0, 1, 3, 4, 2)
    pool_v = cache_v.transpose(0, 1, 3, 4, 2)
    pool_ki = cache_idx_k.transpose(0, 1, 3, 2)

    nb_all = bp + bs
    nb_pad = -(-nb_all // 8) * 8
    c_all = jnp.pad(jnp.concatenate([c_prompt, c_sample], 0), ((0, nb_pad - nb_all), (0, 0)))
    mod = _ada_mod(c_all, w_ada, b_ada)
    mod_p = mod[:, :, :bp].reshape(depth, 6, bp, 1, d)
    mod_s = mod[:, :, bp:nb_all].reshape(depth, 6, bs, 1, d)

    tt_p = _pick_tile(tp, 512)
    tabs_p = [a.reshape(tp // tt_p, tt_p, LANE) for a in _rope_tables(jnp.arange(tp, dtype=I32))]
    bb_s = _pick_tile(bs, 32)
    tabs_s = [jnp.tile(a, (bb_s, 1)).reshape(1, bb_s * ts, LANE)
              for a in _rope_tables(past + jnp.arange(ts, dtype=I32))]

    tt_lru = _pick_tile(tp, 256)
    tq = _pick_tile(tp, Q_BLOCK)
    n_ranges = max(r for r in (1, 2, 4, 8, 16) if (tp // tq) % r == 0)
    tn_p = _pick_tile(tp, 512)
    bb_peer = _pick_tile(bs, max(1, 512 // ts))

    yp, ys = x_prompt, x_sample
    outs = {k: [] for k in ("kp", "vp", "kip", "cp", "hp", "ks", "vs", "kis", "cs", "hs")}
    for l in range(depth):
        lru_w = (conv_w[l], conv_b[l][None], wa_hi[l], wa_lo[l], lru_ba[l][None], wx_hi[l], wx_lo[l],
                 lru_bx[l][None], lru_lambda[l][None], beta_lru[l][None])
        zx, zg, q, k, v, qi, tail, ki, ki3, k16, v16 = _in_proj(yp, mod_p, l, w_in_hi, w_in_lo,
                                                                tabs_p, 1, tt_p)
        zc = jnp.zeros((bp, CONV_W - 1, lw), F32)
        zh = jnp.zeros((bp, lw), F32)
        mix_lru, nconv, nh = _lru_prompt(zx, zg, zc, zh, lru_w, bp, tp, tt_lru)
        o = _attn_prompt(qi, tail, ki3, q, k16, v16, bp, tp, tq, n_ranges)
        x1 = _out_proj(yp, mod_p, l, mix_lru, o, beta_attn[l][None], w_out_hi, w_out_lo,
                       ln1_g[l][None], ln1_b[l][None], 1, tt_p)
        yp = _peer(x1, mod_p, l, wq_hi, wq_lo, pk3, u_bf, v_bf,
                   ln2_g[l][None], ln2_b[l][None], 1, tn_p, eb)
        outs["kp"].append(k.reshape(bp, tp, N_HEADS, HEAD_DIM))
        outs["vp"].append(v.reshape(bp, tp, N_HEADS, HEAD_DIM))
        outs["kip"].append(ki.reshape(bp, tp, IDX_DIM))
        outs["cp"].append(nconv)
        outs["hp"].append(nh.reshape(bp, lw))
        zx, zg, q, k, v, qi, tail, ki = _in_proj(ys, mod_s, l, w_in_hi, w_in_lo, tabs_s, bb_s, ts)[:8]
        tm = lambda a: a.reshape(bs, ts, lw).transpose(1, 0, 2)
        mix_t, nconv_t, nh = _lru_sample(tm(zx), tm(zg), state_conv[l].transpose(1, 0, 2),
                                         state_lru[l], lru_w)
        mix_lru = mix_t.transpose(1, 0, 2).reshape(bs * ts, lw)
        o = _attn_sample(page_table, qi, tail, q, k, v, pool_ki, pool_k, pool_v, l, bs, ts)
        x1 = _out_proj(ys, mod_s, l, mix_lru, o, beta_attn[l][None], w_out_hi, w_out_lo,
                       ln1_g[l][None], ln1_b[l][None], bb_s, ts)
        ys = _peer(x1, mod_s, l, wq_hi, wq_lo, pk3, u_bf, v_bf,
                   ln2_g[l][None], ln2_b[l][None], bb_peer, ts, eb)
        outs["ks"].append(k.reshape(bs, ts, N_HEADS, HEAD_DIM))
        outs["vs"].append(v.reshape(bs, ts, N_HEADS, HEAD_DIM))
        outs["kis"].append(ki.reshape(bs, ts, IDX_DIM))
        outs["cs"].append(nconv_t.transpose(1, 0, 2))
        outs["hs"].append(nh)

    st = lambda name: jnp.stack(outs[name])
    return (yp, ys, st("kp"), st("vp"), st("kip"), st("cp"), st("hp"),
            st("ks"), st("vs"), st("kis"), st("cs"), st("hs"))
```

```python
import functools
import math

import jax
import jax.numpy as jnp
from jax import lax
from jax.experimental import pallas as pl
from jax.experimental.pallas import tpu as pltpu

F32 = jnp.float32
BF16 = jnp.bfloat16
I32 = jnp.int32

LRU_C = 8.0
CONV_W = 4
N_HEADS = 8
HEAD_DIM = 64
IDX_HEADS = 4
IDX_DIM = 64
TOPK_MAX = 256
ROPE_THETA = 500000.0
ROPE_FRAC_DIV = 4
Q_BLOCK = 128
PEER_HEADS = 8
PEER_TOPK = 16
DEPTH_NOMINAL = 4
ALPHA = (2 * DEPTH_NOMINAL) ** 0.25
LN_EPS = 1e-5

LANE = 128
VMEM_LIMIT = 56 * 1024 * 1024

NN = (((1,), (0,)), ((), ()))
NT = (((1,), (1,)), ((), ()))
TN = (((0,), (0,)), ((), ()))

NEG_INF = float("-inf")
INT_MIN = -2147483648
KEY_NEG_INF = -2139095041


def _cparams(*sem):
    return pltpu.CompilerParams(dimension_semantics=sem, vmem_limit_bytes=VMEM_LIMIT)


def _split(a):
    hi = a.astype(BF16)
    lo = (a - hi.astype(F32)).astype(BF16)
    return hi, lo


def _dot(a, b, dims=NN):
    return lax.dot_general(a, b, dims, preferred_element_type=F32)


def _dot3s(ah, al, bh, bl, dims=NN):
    return _dot(ah, bh, dims) + (_dot(ah, bl, dims) + _dot(al, bh, dims))


def _dot3w(a, bh, bl, dims=NN):
    ah, al = _split(a)
    return _dot3s(ah, al, bh, bl, dims)


def _cat3(x, lhs):
    hi = x.astype(BF16).astype(F32)
    lo = (x - hi).astype(BF16).astype(F32)
    parts = [hi, hi, lo] if lhs else [hi, lo, hi]
    return jnp.concatenate(parts, 1).astype(BF16)


def _gelu(x):
    return 0.5 * x * (1.0 + lax.erf(x * (1.0 / math.sqrt(2.0))))


def _layer_norm(y, g, b):
    mu = jnp.mean(y, -1, keepdims=True)
    d = y - mu
    var = jnp.mean(d * d, -1, keepdims=True)
    return d * lax.rsqrt(var + LN_EPS) * g + b


def _rms(y, g):
    return y * lax.rsqrt(jnp.mean(y * y, -1, keepdims=True) + LN_EPS) * g


def _order_key(x):
    b = pltpu.bitcast(x, I32)
    return jnp.where(b < 0, b ^ 0x7FFFFFFF, b)


def _ada_kernel(c_ref, w_ref, b_ref, o_ref):
    c = c_ref[...]
    s = c * jax.nn.sigmoid(c)
    wh, wl = _split(w_ref[...])
    o_ref[...] = _dot3w(s, wh, wl) + b_ref[...]


def _ada_mod(c_all, w_ada, b_ada):
    depth, d, six_d = w_ada.shape
    nb = c_all.shape[0]
    return pl.pallas_call(
        _ada_kernel,
        grid=(depth, six_d // d),
        in_specs=[
            pl.BlockSpec((nb, d), lambda l, j: (0, 0)),
            pl.BlockSpec((None, d, d), lambda l, j: (l, 0, j)),
            pl.BlockSpec((None, None, 1, d), lambda l, j: (l, j, 0, 0)),
        ],
        out_specs=pl.BlockSpec((None, None, nb, d), lambda l, j: (l, j, 0, 0)),
        out_shape=jax.ShapeDtypeStruct((depth, six_d // d, nb, d), F32),
        compiler_params=_cparams("arbitrary", "arbitrary"),
        name="ada_mod",
    )(c_all, w_ada, b_ada.reshape(depth, six_d // d, 1, d))


def _rope(x, c, s1, s2):
    w = x.shape[-1]
    return x * c + pltpu.roll(x, 8, 1) * s1 + pltpu.roll(x, w - 8, 1) * s2


def _inproj_kernel(x_ref, sc_ref, sh_ref, wh_ref, wl_ref, rc_ref, rs1_ref, rs2_ref,
                   tc_ref, ts1_ref, ts2_ref,
                   zx_ref, zg_ref, q_ref, k_ref, v_ref, qi_ref, tail_ref, ki_ref, ki3_ref,
                   kb_ref, vb_ref):
    bb, tt, d = x_ref.shape
    h = x_ref[...] * (1.0 + sc_ref[...]) + sh_ref[...]
    hh, hl = _split(h.reshape(bb * tt, d))

    def proj(lo, hi):
        return _dot3s(hh, hl, wh_ref[:, lo:hi], wl_ref[:, lo:hi])

    def proj1(lo, hi):
        return _dot(hh, wh_ref[:, lo:hi])

    lw = zx_ref.shape[-1]
    aw = q_ref.shape[-1]
    iw = qi_ref.shape[-1]
    o = 0
    zx_ref[...] = proj1(o, o + lw); o += lw
    zg_ref[...] = proj1(o, o + lw); o += lw
    rc, rs1, rs2 = rc_ref[...], rs1_ref[...], rs2_ref[...]
    rep = aw // LANE
    c4 = jnp.concatenate([rc] * rep, 1)
    s14 = jnp.concatenate([rs1] * rep, 1)
    s24 = jnp.concatenate([rs2] * rep, 1)
    q_ref[...] = _rope(proj1(o, o + aw), c4, s14, s24); o += aw
    kk = _rope(proj1(o, o + aw), c4, s14, s24); o += aw
    k_ref[...] = kk
    kb_ref[...] = kk.astype(BF16)
    vv = proj1(o, o + aw); o += aw
    v_ref[...] = vv
    vb_ref[...] = vv.astype(BF16)
    repi = iw // LANE
    qi_ref[...] = _rope(proj(o, o + iw), jnp.concatenate([rc] * repi, 1),
                        jnp.concatenate([rs1] * repi, 1), jnp.concatenate([rs2] * repi, 1)); o += iw
    tail = _rope(proj(o, o + LANE), tc_ref[...], ts1_ref[...], ts2_ref[...])
    tail_ref[...] = tail
    ki_ref[...] = tail[:, :IDX_DIM]
    ki3_ref[...] = _cat3(tail[:, :IDX_DIM], lhs=False)


def _rope_tables(pos):
    t = pos.shape[0]
    rot = HEAD_DIM // ROPE_FRAC_DIV
    half = rot // 2
    freqs = ROPE_THETA ** (-jnp.arange(half, dtype=F32) / half)
    ang = pos.astype(F32)[:, None] * freqs[None, :]
    cos, sin = jnp.cos(ang), jnp.sin(ang)
    ones = jnp.ones((t, HEAD_DIM - rot), F32)
    zeros = jnp.zeros((t, HEAD_DIM - rot), F32)
    zh = jnp.zeros((t, half), F32)
    c64 = jnp.concatenate([cos, cos, ones], 1)
    s1_64 = jnp.concatenate([zh, sin, zeros], 1)
    s2_64 = jnp.concatenate([-sin, zh, zeros], 1)
    rc = jnp.concatenate([c64, c64], 1)
    rs1 = jnp.concatenate([s1_64, s1_64], 1)
    rs2 = jnp.concatenate([s2_64, s2_64], 1)
    wscale = jnp.concatenate([jnp.full((t, IDX_HEADS), IDX_HEADS ** -0.5, F32),
                              jnp.ones((t, LANE - IDX_DIM - IDX_HEADS), F32)], 1)
    z64 = jnp.zeros((t, LANE - IDX_DIM), F32)
    tc = jnp.concatenate([c64, wscale], 1)
    ts1 = jnp.concatenate([s1_64, z64], 1)
    ts2 = jnp.concatenate([s2_64, z64], 1)
    return rc, rs1, rs2, tc, ts1, ts2


def _in_proj(x, mod, layer, w_hi, w_lo, tables, bb, tt):
    b, t, d = x.shape
    n = b * t
    rows = bb * tt
    nt = t // tt
    lw = d // 2
    aw = N_HEADS * HEAD_DIM
    iw = IDX_HEADS * IDX_DIM
    cols = w_hi.shape[-1]
    tab_idx = (lambda bi, ti: (ti, 0, 0)) if tables[0].shape[0] > 1 else (lambda bi, ti: (0, 0, 0))
    tab_spec = pl.BlockSpec((None, rows, LANE), tab_idx)
    row_idx = lambda bi, ti: (bi * nt + ti, 0)

    def mod_spec(j):
        return pl.BlockSpec((None, None, bb, 1, d), lambda bi, ti: (layer, j, bi, 0, 0))

    outs = [(n, lw), (n, lw), (n, aw), (n, aw), (n, aw), (n, iw), (n, LANE), (n, IDX_DIM),
            (n, 3 * IDX_DIM), (n, aw), (n, aw)]
    dtypes = [F32] * 8 + [BF16] * 3
    return pl.pallas_call(
        _inproj_kernel,
        grid=(b // bb, nt),
        in_specs=[
            pl.BlockSpec((bb, tt, d), lambda bi, ti: (bi, ti, 0)),
            mod_spec(1), mod_spec(0),
            pl.BlockSpec((None, d, cols), lambda bi, ti: (layer, 0, 0)),
            pl.BlockSpec((None, d, cols), lambda bi, ti: (layer, 0, 0)),
        ] + [tab_spec] * 6,
        out_specs=[pl.BlockSpec((rows, w), row_idx) for _, w in outs],
        out_shape=[jax.ShapeDtypeStruct(s, dt) for s, dt in zip(outs, dtypes)],
        compiler_params=_cparams("arbitrary", "arbitrary"),
        name="in_proj",
    )(x, mod, mod, w_hi, w_lo, *tables)


def _lru_gates(xc, wah, wal, ba, wxh, wxl, bx, lam):
    xh, xl = _split(xc)
    r = jax.nn.sigmoid(_dot3s(xh, xl, wah, wal) + ba)
    i = jax.nn.sigmoid(_dot3s(xh, xl, wxh, wxl) + bx)
    nl = -lam
    sp = jnp.maximum(nl, 0.0) + jnp.log1p(jnp.exp(-jnp.abs(nl)))
    log_a = -LRU_C * r * sp
    a = jnp.exp(log_a)
    u = jnp.sqrt(1.0 - a * a) * (i * xc)
    return a, u


def _lru_prompt_kernel(zx_ref, zg_ref, cbuf_ref, h0_ref, cw_ref, cb_ref, wah_ref, wal_ref, ba_ref,
                       wxh_ref, wxl_ref, bx_ref, lam_ref, beta_ref,
                       mix_ref, nconv_ref, nh_ref, ext_ref, hst_ref):
    tt = zx_ref.shape[0]
    ti = pl.program_id(1)

    @pl.when(ti == 0)
    def _():
        ext_ref[5:8, :] = cbuf_ref[...]
        hst_ref[0:1, :] = h0_ref[...]

    x = zx_ref[...]
    ext_ref[8:8 + tt, :] = x
    w = cw_ref[...]
    xc = (cb_ref[...] + ext_ref[5:5 + tt, :] * w[0:1] + ext_ref[6:6 + tt, :] * w[1:2]
          + ext_ref[7:7 + tt, :] * w[2:3] + x * w[3:4])
    tail3 = ext_ref[tt + 5:tt + 8, :]
    ext_ref[5:8, :] = tail3

    a, u = _lru_gates(xc, wah_ref[...], wal_ref[...], ba_ref[...], wxh_ref[...], wxl_ref[...],
                      bx_ref[...], lam_ref[...])
    row = lax.broadcasted_iota(I32, (tt, 1), 0)
    s = 1
    while s < tt:
        a_sh = pltpu.roll(a, s, 0)
        u_sh = pltpu.roll(u, s, 0)
        m = row >= s
        u = u + a * jnp.where(m, u_sh, 0.0)
        a = a * jnp.where(m, a_sh, 1.0)
        s *= 2
    h = a * hst_ref[0:1, :] + u
    hst_ref[0:1, :] = h[tt - 1:tt, :]
    y = _gelu(zg_ref[...]) * h
    mix_ref[...] = _rms(y, beta_ref[...])

    @pl.when(ti == pl.num_programs(1) - 1)
    def _():
        nconv_ref[...] = tail3
        nh_ref[...] = h[tt - 1:tt, :]


def _lru_prompt(zx, zg, cbuf, h0, lw, b, t, tt):
    n, w = zx.shape
    nt = t // tt
    row_spec = pl.BlockSpec((tt, w), lambda bi, ti: (bi * nt + ti, 0))
    full = lambda shape: pl.BlockSpec(shape, lambda bi, ti: tuple(0 for _ in shape))
    vec = full((1, w))
    mat = full((w, w))
    return pl.pallas_call(
        _lru_prompt_kernel,
        grid=(b, nt),
        in_specs=[row_spec, row_spec,
                  pl.BlockSpec((None, CONV_W - 1, w), lambda bi, ti: (bi, 0, 0)),
                  pl.BlockSpec((None, 1, w), lambda bi, ti: (bi, 0, 0)),
                  full((CONV_W, w)), vec, mat, mat, vec, mat, mat, vec, vec, vec],
        out_specs=[row_spec,
                   pl.BlockSpec((None, CONV_W - 1, w), lambda bi, ti: (bi, 0, 0)),
                   pl.BlockSpec((None, 1, w), lambda bi, ti: (bi, 0, 0))],
        out_shape=[jax.ShapeDtypeStruct((n, w), F32),
                   jax.ShapeDtypeStruct((b, CONV_W - 1, w), F32),
                   jax.ShapeDtypeStruct((b, 1, w), F32)],
        scratch_shapes=[pltpu.VMEM((tt + 8, w), F32), pltpu.VMEM((8, w), F32)],
        compiler_params=_cparams("arbitrary", "arbitrary"),
        name="lru_prompt",
    )(zx, zg, cbuf, h0.reshape(b, 1, w), *lw)


def _lru_sample_kernel(zx_ref, zg_ref, cbuf_ref, h0_ref, cw_ref, cb_ref, wah_ref, wal_ref, ba_ref,
                       wxh_ref, wxl_ref, bx_ref, lam_ref, beta_ref,
                       mix_ref, nconv_ref, nh_ref, xc_ref):
    t, b, wd = zx_ref.shape
    w = cw_ref[...]
    xp = [cbuf_ref[k] for k in range(CONV_W - 1)] + [zx_ref[k] for k in range(t)]
    for k in range(t):
        xc_ref[k * b:(k + 1) * b, :] = (cb_ref[...] + xp[k] * w[0:1] + xp[k + 1] * w[1:2]
                                        + xp[k + 2] * w[2:3] + xp[k + 3] * w[3:4])
    for k in range(CONV_W - 1):
        nconv_ref[k] = xp[t + k]
    a, u = _lru_gates(xc_ref[...], wah_ref[...], wal_ref[...], ba_ref[...], wxh_ref[...],
                      wxl_ref[...], bx_ref[...], lam_ref[...])
    h = h0_ref[...]
    beta = beta_ref[...]
    for k in range(t):
        h = a[k * b:(k + 1) * b, :] * h + u[k * b:(k + 1) * b, :]
        mix_ref[k] = _rms(_gelu(zg_ref[k]) * h, beta)
    nh_ref[...] = h


def _lru_sample(zx_t, zg_t, cbuf_t, h0, lw):
    t, b, w = zx_t.shape
    full = lambda shape: pl.BlockSpec(shape, lambda i: tuple(0 for _ in shape))
    vec = full((1, w))
    mat = full((w, w))
    return pl.pallas_call(
        _lru_sample_kernel,
        grid=(1,),
        in_specs=[full((t, b, w)), full((t, b, w)), full((CONV_W - 1, b, w)), full((b, w)),
                  full((CONV_W, w)), vec, mat, mat, vec, mat, mat, vec, vec, vec],
        out_specs=[full((t, b, w)), full((CONV_W - 1, b, w)), full((b, w))],
        out_shape=[jax.ShapeDtypeStruct((t, b, w), F32),
                   jax.ShapeDtypeStruct((CONV_W - 1, b, w), F32),
                   jax.ShapeDtypeStruct((b, w), F32)],
        scratch_shapes=[pltpu.VMEM((t * b, w), F32)],
        compiler_params=_cparams("arbitrary"),
        name="lru_sample",
    )(zx_t, zg_t, cbuf_t, h0, *lw)


def _select_bias(score, qpos, key_ref, topk):
    thr, res = _select_threshold(score, qpos, key_ref, topk)
    return _select_finish(qpos, key_ref, thr, res, topk)


def _select_threshold(score, qpos, key_ref, topk):
    q, s = score.shape
    kpos = lax.broadcasted_iota(I32, (q, s), 1)
    vis = kpos <= qpos
    key_ref[...] = jnp.where(vis, _order_key(score + 0.0), KEY_NEG_INF)

    kf = float(topk)

    ngrp = 2 if q % (2 * SUBLANES) == 0 else 1
    rg = q // ngrp
    unroll = 4

    def count_ge(g, t):
        keys = key_ref[g * rg:(g + 1) * rg, :]
        return jnp.sum(jnp.where(keys >= t, 1.0, 0.0), axis=1, keepdims=True)

    init = []
    for g in range(ngrp):
        c0 = count_ge(g, jnp.zeros((rg, 1), I32))
        small = (qpos[g * rg:(g + 1) * rg] + 1) <= topk
        init.append(jnp.where(small, KEY_NEG_INF + 1, jnp.where(c0 >= kf, 0, INT_MIN)).astype(I32))
        init.append(jnp.where(small, 1, jnp.where(c0 == kf, 1, 0)).astype(I32))

    def unresolved(state):
        m = state[1]
        for g in range(1, ngrp):
            m = jnp.minimum(m, state[2 * g + 1])
        return jnp.min(m) == 0

    def vcond(c):
        return jnp.logical_and(c[0] < 31, unresolved(c[1:]))

    def vbody(c):
        i, state = c[0], list(c[1:])
        for u in range(unroll):
            bit = jnp.where(i + u < 31, jnp.left_shift(jnp.int32(1), jnp.maximum(30 - i - u, 0)), 0)
            for g in range(ngrp):
                t, res = state[2 * g], state[2 * g + 1]
                cand = t | bit
                cnt = count_ge(g, cand)
                state[2 * g] = jnp.where(res > 0, t, jnp.where(cnt >= kf, cand, t))
                state[2 * g + 1] = jnp.where(cnt == kf, 1, res)
        return (i + unroll, *state)

    if q > SUBLANES:
        out = lax.while_loop(vcond, vbody, (jnp.int32(0), *init))
        thr = jnp.concatenate([out[1 + 2 * g] for g in range(ngrp)], 0)
        res = jnp.concatenate([out[2 + 2 * g] for g in range(ngrp)], 0)
    else:
        keys = key_ref[...]
        thr, res = init
        hi = 31
        while hi > 0:
            nb = min(RADIX_BITS, hi)
            shift = hi - nb
            best_t, best_c = thr, jnp.full((q, 1), -1.0, F32)
            for j in range(1, 1 << nb):
                cand = thr | (j << shift)
                cnt = jnp.sum(jnp.where(keys >= cand, 1.0, 0.0), axis=1, keepdims=True)
                ok = cnt >= kf
                best_t = jnp.where(ok, cand, best_t)
                best_c = jnp.where(ok, cnt, best_c)
            thr = jnp.where(res > 0, thr, best_t)
            res = jnp.where(best_c == kf, 1, res)
            hi = shift
    return thr, res


def _select_finish(qpos, key_ref, thr, res, topk):
    q, s = key_ref.shape
    kf = float(topk)
    kpos = lax.broadcasted_iota(I32, (q, s), 1)
    vis = kpos <= qpos
    nbits = max(1, (s - 1).bit_length())

    def tie_phase():
        c_gt = jnp.sum(jnp.where(key_ref[...] > thr, 1.0, 0.0), axis=1, keepdims=True)
        need = kf - c_gt

        def ibody(i, j):
            cand = j + jnp.left_shift(jnp.int32(1), nbits - 1 - i)
            kp = lax.broadcasted_iota(I32, (q, s), 1)
            hit = jnp.where(key_ref[...] == thr, jnp.where(kp <= cand, 1.0, 0.0), 0.0)
            f = jnp.sum(hit, axis=1, keepdims=True)
            return jnp.where(f < need, cand, j)

        return lax.fori_loop(0, nbits, ibody, jnp.full((q, 1), -1, I32)) + 1

    jsel = lax.cond(jnp.min(res) == 0, tie_phase, lambda: jnp.full((q, 1), s, I32))
    jsel = jnp.where(res > 0, s, jsel)
    key = key_ref[...]
    sel = jnp.where(key > thr, 1, jnp.where(key == thr, jnp.where(kpos <= jsel, 1, 0), 0))
    sel = jnp.where(vis, sel, 0)
    return jnp.where(sel > 0, 0.0, NEG_INF)


def _attn_prompt_kernel(qi_ref, tailq_ref, ki3_ref, q_ref, k_ref, v_ref, o_ref, bias_ref, key_ref,
                        *, topk, qb0):
    tq = q_ref.shape[0]
    ki3 = ki3_ref[...]
    tqv = tailq_ref[...]
    qi = qi_ref[...]
    score = None
    for h in range(IDX_HEADS):
        lg = _dot(_cat3(qi[:, h * IDX_DIM:(h + 1) * IDX_DIM], lhs=True), ki3, NT)
        term = tqv[:, IDX_DIM + h:IDX_DIM + h + 1] * jnp.maximum(lg, 0.0)
        score = term if score is None else score + term
    qpos = (qb0 + pl.program_id(1)) * tq + lax.broadcasted_iota(I32, (tq, 1), 0)
    bias_ref[0:tq, :] = _select_bias(score, qpos, key_ref, topk)
    bias_ref[tq:2 * tq, :] = bias_ref[0:tq, :]

    low = lax.broadcasted_iota(I32, (tq, LANE), 1) < HEAD_DIM
    for g in range(q_ref.shape[1] // LANE):
        cs = slice(g * LANE, (g + 1) * LANE)
        qg = q_ref[:, cs] * (HEAD_DIM ** -0.5)
        qpair = jnp.concatenate([jnp.where(low, qg, 0.0), jnp.where(low, 0.0, qg)], 0).astype(BF16)
        s = _dot(qpair, k_ref[:, cs], NT) + bias_ref[...]
        m = jnp.max(s, axis=1, keepdims=True)
        p = jnp.exp(s - m)
        l = jnp.sum(p, axis=1, keepdims=True)
        og = _dot(p.astype(BF16), v_ref[:, cs]) / l
        o_ref[:, cs] = jnp.where(low, og[0:tq], og[tq:2 * tq])


def _attn_prompt(qi, tail, ki3, q, k_bf, v_bf, b, t, tq, n_ranges):
    nq = t // tq
    topk = min(TOPK_MAX, t // 4)
    iw = qi.shape[-1]
    aw = q.shape[-1]
    nqr = nq // n_ranges
    keys3 = lambda a: a.reshape(b, t, a.shape[-1])
    outs = []
    for r in range(n_ranges):
        qb0 = r * nqr
        tv = (r + 1) * nqr * tq
        qrow = lambda w, qb0=qb0: pl.BlockSpec((tq, w), lambda bi, qb: (bi * nq + qb0 + qb, 0))
        kvis = lambda w, tv=tv: pl.BlockSpec((None, tv, w), lambda bi, qb: (bi, 0, 0))
        outs.append(pl.pallas_call(
            functools.partial(_attn_prompt_kernel, topk=topk, qb0=qb0),
            grid=(b, nqr),
            in_specs=[qrow(iw), qrow(LANE), kvis(3 * IDX_DIM), qrow(aw), kvis(aw), kvis(aw)],
            out_specs=pl.BlockSpec((None, tq, aw), lambda bi, qb: (bi, qb, 0)),
            out_shape=jax.ShapeDtypeStruct((b, nqr * tq, aw), F32),
            scratch_shapes=[pltpu.VMEM((2 * tq, tv), F32), pltpu.VMEM((tq, tv), I32)],
            compiler_params=_cparams("arbitrary", "arbitrary"),
            name="attn_prompt",
        )(qi, tail, keys3(ki3), q, keys3(k_bf), keys3(v_bf)))
    return jnp.concatenate(outs, 1).reshape(b * t, aw)


def _attn_sample_kernel(pt_ref, qi_ref, tail_ref, q_ref, kn_ref, vn_ref, *rest, n_pages, page, topk, ne):
    npg = ne * n_pages
    ki_pages, k_pages, v_pages = rest[:npg], rest[npg:2 * npg], rest[2 * npg:3 * npg]
    o_ref, kit_ref, kst_ref, vst_ref, key_ref = rest[3 * npg:]
    t = q_ref.shape[0] // ne
    past = n_pages * page
    hsl = lambda h: slice(h * HEAD_DIM, (h + 1) * HEAD_DIM)
    qpos = past + lax.broadcasted_iota(I32, (t, 1), 0)
    zpad = jnp.zeros((page - t, HEAD_DIM), F32)
    rpad = -t % 16

    found = []
    for e in range(ne):
        rows = slice(e * t, (e + 1) * t)
        for j in range(n_pages):
            cols = slice(j * page, (j + 1) * page)
            kit = ki_pages[e * n_pages + j][...]
            hi = kit.astype(BF16)
            lo = (kit - hi.astype(F32)).astype(BF16)
            kit_ref[e, 0:IDX_DIM, cols] = hi
            kit_ref[e, IDX_DIM:2 * IDX_DIM, cols] = lo
            kit_ref[e, 2 * IDX_DIM:3 * IDX_DIM, cols] = hi
            kst_ref[e, :, :, cols] = k_pages[e * n_pages + j][...].astype(BF16)
            vst_ref[e, :, :, cols] = v_pages[e * n_pages + j][...].astype(BF16)
        tl = tail_ref[rows, :]
        ki_new = _cat3(jnp.concatenate([tl[:, :IDX_DIM], jnp.zeros((page - t, IDX_DIM), F32)], 0),
                       lhs=False)
        qi = qi_ref[rows, :]
        qi_rows = jnp.concatenate([qi[:, h * IDX_DIM:(h + 1) * IDX_DIM] for h in range(IDX_HEADS)], 0)
        w_rows = jnp.concatenate([tl[:, IDX_DIM + h:IDX_DIM + h + 1] for h in range(IDX_HEADS)], 0)
        qi3 = _cat3(qi_rows, lhs=True)
        logits = jnp.concatenate([_dot(qi3, kit_ref[e]), _dot(qi3, ki_new, NT)], 1)
        sc = jnp.maximum(logits, 0.0) * w_rows
        score = sc[0:t]
        for h in range(1, IDX_HEADS):
            score = score + sc[h * t:(h + 1) * t]
        found.append(_select_threshold(score, qpos, key_ref.at[e], topk))

    biases = [_select_finish(qpos, key_ref.at[e], *found[e], topk) for e in range(ne)]

    for e in range(ne):
        rows = slice(e * t, (e + 1) * t)
        q = q_ref[rows, :] * (HEAD_DIM ** -0.5)
        kn = kn_ref[rows, :]
        vn = vn_ref[rows, :]
        outs = []
        for h in range(N_HEADS):
            qh = jnp.concatenate([q[:, hsl(h)], jnp.zeros((rpad, HEAD_DIM), F32)], 0).astype(BF16)
            k_new = jnp.concatenate([kn[:, hsl(h)], zpad], 0).astype(BF16)
            v_new = jnp.concatenate([vn[:, hsl(h)], zpad], 0).astype(BF16)
            s = jnp.concatenate([_dot(qh, kst_ref[e, h]), _dot(qh, k_new, NT)], 1)[0:t] + biases[e]
            m = jnp.max(s, axis=1, keepdims=True)
            p = jnp.exp(s - m)
            l = jnp.sum(p, axis=1, keepdims=True)
            ph = jnp.concatenate([p, jnp.zeros((rpad, past + page), F32)], 0).astype(BF16)
            o = _dot(ph[:, :past], vst_ref[e, h], NT) + _dot(ph[:, past:], v_new)
            outs.append(o[0:t] / l)
        o_ref[rows, :] = jnp.concatenate(outs, 1)


def _attn_sample(page_table, qi, tail, q, kn, vn, pool_ki, pool_k, pool_v, layer, b, t):
    n_pages = page_table.shape[1]
    page = pool_ki.shape[3]
    past = n_pages * page
    topk = min(TOPK_MAX, (past + t) // 4)
    aw = q.shape[-1]
    iw = qi.shape[-1]
    s_pad = past + page
    ne = 2 if b % 2 == 0 else 1
    row = lambda w: pl.BlockSpec((ne * t, w), lambda bi, pt: (bi, 0))
    slots = [(e, j) for e in range(ne) for j in range(n_pages)]

    def ki_spec(e, j):
        return pl.BlockSpec((None, None, IDX_DIM, page),
                            lambda bi, pt: (layer, pt[ne * bi + e, j], 0, 0))

    def kv_spec(e, j):
        return pl.BlockSpec((None, None, N_HEADS, HEAD_DIM, page),
                            lambda bi, pt: (layer, pt[ne * bi + e, j], 0, 0, 0))

    in_specs = [row(iw), row(LANE), row(aw), row(aw), row(aw)]
    in_specs += [ki_spec(e, j) for e, j in slots]
    in_specs += [kv_spec(e, j) for e, j in slots]
    in_specs += [kv_spec(e, j) for e, j in slots]
    grid_spec = pltpu.PrefetchScalarGridSpec(
        num_scalar_prefetch=1,
        grid=(b // ne,),
        in_specs=in_specs,
        out_specs=row(aw),
        scratch_shapes=[pltpu.VMEM((ne, 3 * IDX_DIM, past), BF16),
                        pltpu.VMEM((ne, N_HEADS, HEAD_DIM, past), BF16),
                        pltpu.VMEM((ne, N_HEADS, HEAD_DIM, past), BF16),
                        pltpu.VMEM((ne, t, s_pad), I32)],
    )
    npg = len(slots)
    return pl.pallas_call(
        functools.partial(_attn_sample_kernel, n_pages=n_pages, page=page, topk=topk, ne=ne),
        grid_spec=grid_spec,
        out_shape=jax.ShapeDtypeStruct((b * t, aw), F32),
        compiler_params=_cparams("arbitrary"),
        name="attn_sample",
    )(page_table, qi, tail, q, kn, vn, *([pool_ki] * npg), *([pool_k] * npg), *([pool_v] * npg))


def _outproj_kernel(x_ref, g_ref, ml_ref, o_ref, beta_ref, wh_ref, wl_ref, lg_ref, lb_ref, y_ref):
    bb, tt, d = x_ref.shape
    lw = ml_ref.shape[-1]
    mo = _rms(o_ref[...], beta_ref[...])
    acc = (_dot3w(ml_ref[...], wh_ref[0:lw, :], wl_ref[0:lw, :])
           + _dot3w(mo, wh_ref[lw:, :], wl_ref[lw:, :]))
    y = ALPHA * x_ref[...] + g_ref[...] * acc.reshape(bb, tt, d)
    y_ref[...] = _layer_norm(y, lg_ref[...], lb_ref[...])


def _out_proj(x, mod, layer, mix_lru, o, beta_attn, w_hi, w_lo, ln_g, ln_b, bb, tt):
    b, t, d = x.shape
    nt = t // tt
    rows = bb * tt
    lw = mix_lru.shape[-1]
    aw = o.shape[-1]
    xspec = pl.BlockSpec((bb, tt, d), lambda bi, ti: (bi, ti, 0))
    vec = lambda w: pl.BlockSpec((1, w), lambda bi, ti: (0, 0))
    wspec = pl.BlockSpec((None, lw + aw, d), lambda bi, ti: (layer, 0, 0))
    return pl.pallas_call(
        _outproj_kernel,
        grid=(b // bb, nt),
        in_specs=[xspec,
                  pl.BlockSpec((None, None, bb, 1, d), lambda bi, ti: (layer, 2, bi, 0, 0)),
                  pl.BlockSpec((rows, lw), lambda bi, ti: (bi * nt + ti, 0)),
                  pl.BlockSpec((rows, aw), lambda bi, ti: (bi * nt + ti, 0)),
                  vec(aw), wspec, wspec, vec(d), vec(d)],
        out_specs=xspec,
        out_shape=jax.ShapeDtypeStruct((b, t, d), F32),
        compiler_params=_cparams("arbitrary", "arbitrary"),
        name="out_proj",
    )(x, mod, mix_lru, o, beta_attn, w_hi, w_lo, ln_g, ln_b)


SUBLANES = 8
RADIX_BITS = 4


def _bitonic_merge(seq):
    seq = list(seq)
    n = len(seq)
    d = n // 2
    while d >= 1:
        for i in range(n):
            if i & d == 0:
                a, b = seq[i], seq[i + d]
                seq[i], seq[i + d] = jnp.maximum(a, b), jnp.minimum(a, b)
        d //= 2
    return seq


def _sort_desc(seq):
    if len(seq) == 1:
        return list(seq)
    half = len(seq) // 2
    return _bitonic_merge(_sort_desc(seq[:half]) + _sort_desc(seq[half:])[::-1])


def _merge_top(a, b, cap):
    n = len(a)
    if 2 * n <= cap:
        return _bitonic_merge(a + b[::-1])
    return _bitonic_merge([jnp.maximum(a[k], b[n - 1 - k]) for k in range(n)])


def _merge_sublanes(s, cap):
    d = SUBLANES // 2
    while d >= 1:
        s = _merge_top(s, [pltpu.roll(v, SUBLANES - d, 0) for v in s], cap)
        d //= 2
    return s


def _top_sorted(x, cap):
    streams = [x[SUBLANES * k:SUBLANES * (k + 1)] for k in range(x.shape[0] // SUBLANES)]
    return _merge_sublanes(_sort_desc(streams)[:cap], cap)


def _peer_kernel(x_ref, sc_ref, sh_ref, g_ref, wqh_ref, wql_ref, k3_ref, u_ref, v_ref,
                 lg_ref, lb_ref, y_ref,
                 hb_ref, q3_ref, st_ref, e_ref, tau_ref, acc_ref):
    bb, tt, d = x_ref.shape
    tn = bb * tt
    nk = st_ref.shape[1]
    eb = u_ref.shape[0] // nk
    ei = pl.program_id(2)

    @pl.when(ei == 0)
    def _():
        h2 = (x_ref[...] * (1.0 + sc_ref[...]) + sh_ref[...]).reshape(tn, d)
        hh, hl = _split(h2)
        hb_ref[...] = hh
        acc_ref[...] = jnp.zeros_like(acc_ref)

        sub = lax.broadcasted_iota(I32, (SUBLANES, tn), 0)

        def pack(rows):
            out = jnp.broadcast_to(rows[0], (SUBLANES, tn))
            for k in range(1, SUBLANES):
                out = jnp.where(sub == k, jnp.broadcast_to(rows[k], (SUBLANES, tn)), out)
            return out

        q_all = _dot3s(hh, hl, wqh_ref[...], wql_ref[...])
        dk = q_all.shape[1] // (2 * PEER_HEADS)
        for hp in range(2 * PEER_HEADS):
            qh, ql = _split(q_all[:, hp * dk:(hp + 1) * dk])
            q3_ref[hp] = jnp.concatenate([qh, ql, qh], 1)

        def per_head(h, carry):
            sv = []
            for p in range(2):
                hp = 2 * h + p
                s_t = _dot(k3_ref[hp], q3_ref[hp], NT)
                st_ref[hp] = s_t
                sv.append([v[0:1] for v in _top_sorted(s_t, PEER_TOPK)])
            sv0, sv1 = sv
            m0, m1 = sv0[0], sv1[0]
            groups = []
            for g in range(PEER_TOPK // SUBLANES):
                base = pack(sv0[SUBLANES * g:SUBLANES * (g + 1)])
                groups.append([base + sv1[b] for b in range(PEER_TOPK)])
            top = groups[0]
            for g in range(1, len(groups)):
                top = _merge_top(top, groups[g], PEER_TOPK)
            top = [v[0:1] for v in _merge_sublanes(top, PEER_TOPK)]
            mtot = m0 + m1
            z = jnp.exp(top[0] - mtot)
            for r in range(1, PEER_TOPK):
                z = z + jnp.exp(top[r] - mtot)
            tau_ref[pl.ds(h, 1), :] = top[PEER_TOPK - 1]
            inv_z = 1.0 / z
            e_ref[2 * h, :, 0:tn] = jnp.exp(st_ref[2 * h] - m0)
            e_ref[2 * h + 1, :, 0:tn] = jnp.exp(st_ref[2 * h + 1] - m1) * inv_z
            return carry

        lax.fori_loop(0, PEER_HEADS, per_head, 0)

    act = _gelu(_dot(u_ref[...], hb_ref[...], NT))
    group = 2 if eb % 2 == 0 else 1
    cw = LANE if tn % LANE == 0 else tn
    ws = [[None] * (tn // cw) for _ in range(eb)]
    for e0i in range(0, eb, group):
        tiles = [ei * eb + e0i + k for k in range(group)]
        for c in range(tn // cw):
            cs = slice(c * cw, (c + 1) * cw)
            gates = [None] * group
            for h in range(PEER_HEADS):
                s1 = st_ref[2 * h + 1, :, cs]
                e1 = e_ref[2 * h + 1, :, cs]
                tau = tau_ref[h:h + 1, cs]
                for k in range(group):
                    s0 = st_ref[2 * h, pl.ds(tiles[k], 1), :][:, cs]
                    e0 = e_ref[2 * h, pl.ds(tiles[k], 1), 0:tn][:, cs]
                    term = jnp.where(s1 + s0 >= tau, e1 * e0, 0.0)
                    gates[k] = term if gates[k] is None else gates[k] + term
            for k in range(group):
                arow = slice((e0i + k) * nk, (e0i + k + 1) * nk)
                ws[e0i + k][c] = (gates[k] * act[arow, cs]).astype(BF16)
    w_all = jnp.concatenate([jnp.concatenate(row, 1) for row in ws], 0)
    acc_ref[...] += _dot(w_all, v_ref[...], TN)

    @pl.when(ei == pl.num_programs(2) - 1)
    def _():
        y = ALPHA * x_ref[...] + g_ref[...] * acc_ref[...].reshape(bb, tt, d)
        y_ref[...] = _layer_norm(y, lg_ref[...], lb_ref[...])


def _peer(x, mod, layer, wq_hi, wq_lo, k3, u_bf, v_bf, ln_g, ln_b, bb, tt, eb):
    b, t, d = x.shape
    nt = t // tt
    tn = bb * tt
    nhp = 2 * PEER_HEADS
    dk = wq_hi.shape[-1] // nhp
    nk = k3.shape[2]
    xspec = pl.BlockSpec((bb, tt, d), lambda bi, ti, ei: (bi, ti, 0))
    vec = pl.BlockSpec((1, d), lambda bi, ti, ei: (0, 0))

    def mod_spec(j):
        return pl.BlockSpec((None, None, bb, 1, d), lambda bi, ti, ei: (layer, j, bi, 0, 0))

    ne = nk // eb
    const = dict(pipeline_mode=pl.Buffered(1))
    wq_spec = pl.BlockSpec((None, d, nhp * dk), lambda bi, ti, ei: (layer, 0, 0), **const)
    k_spec = pl.BlockSpec((None, nhp, nk, 3 * dk), lambda bi, ti, ei: (layer, 0, 0, 0), **const)
    tab_spec = pl.BlockSpec((None, eb * nk, d), lambda bi, ti, ei: (layer, ei, 0))
    return pl.pallas_call(
        _peer_kernel,
        grid=(b // bb, nt, ne),
        in_specs=[xspec, mod_spec(4), mod_spec(3), mod_spec(5), wq_spec, wq_spec, k_spec,
                  tab_spec, tab_spec, vec, vec],
        out_specs=xspec,
        out_shape=jax.ShapeDtypeStruct((b, t, d), F32),
        scratch_shapes=[pltpu.VMEM((tn, d), BF16), pltpu.VMEM((nhp, tn, 3 * dk), BF16),
                        pltpu.VMEM((nhp, nk, tn), F32), pltpu.VMEM((nhp, nk, tn + LANE), F32),
                        pltpu.VMEM((PEER_HEADS, tn), F32),
                        pltpu.VMEM((tn, d), F32)],
        compiler_params=_cparams("arbitrary", "arbitrary", "arbitrary"),
        name="peer",
    )(x, mod, mod, mod, wq_hi, wq_lo, k3, u_bf, v_bf, ln_g, ln_b)


def _block_diag(w):
    l, nb, bw, _ = w.shape
    eye = jnp.eye(nb, dtype=w.dtype)
    return jnp.einsum("lncd,nm->lncmd", w, eye).reshape(l, nb * bw, nb * bw)


def _pick_tile(t, target):
    tt = min(t, target)
    while t % tt:
        tt //= 2
    return tt


def kernel(x_prompt, x_sample, cache_k, cache_v, cache_idx_k, state_conv, state_lru, page_table,
           c_prompt, c_sample, w_ada, b_ada, w_in, conv_w, conv_b, lru_wa, lru_ba, lru_wx, lru_bx,
           lru_lambda, beta_lru, beta_attn, w_out, ln1_g, ln1_b, ln2_g, ln2_b, peer_wq, peer_keys,
           peer_u, peer_v):
    depth = w_ada.shape[0]
    bp, tp, d = x_prompt.shape
    bs, ts, _ = x_sample.shape
    lw = d // 2
    aw = N_HEADS * HEAD_DIM
    page = cache_k.shape[2]
    past = page_table.shape[1] * page
    assert tp >= CONV_W - 1 and ts >= CONV_W - 1 and ts % 8 == 0 and bs % 8 == 0
    assert cache_k.shape[3:] == (N_HEADS, HEAD_DIM) and page >= ts and HEAD_DIM == 64

    in_cols = w_in.shape[-1]
    cols_pad = -(-in_cols // LANE) * LANE
    w_in_hi, w_in_lo = _split(jnp.pad(w_in, ((0, 0), (0, 0), (0, cols_pad - in_cols))))
    w_out_hi, w_out_lo = _split(w_out)
    wa_hi, wa_lo = _split(_block_diag(lru_wa))
    wx_hi, wx_lo = _split(_block_diag(lru_wx))
    nhp = 2 * PEER_HEADS
    dk = peer_wq.shape[-1] // nhp
    wq_hi, wq_lo = _split(peer_wq)
    nk = peer_keys.shape[3]
    pk = peer_keys.transpose(0, 2, 1, 3, 4).reshape(depth, nhp, nk, dk)
    pk_hi, pk_lo = _split(pk)
    pk3 = jnp.concatenate([pk_hi, pk_hi, pk_lo], -1)
    u_bf = peer_u.astype(BF16)
    v_bf = peer_v.astype(BF16)
    eb = 4 if nk % 4 == 0 else 1
    pool_k = cache_k.transpose(0, 1, 3, 4, 2)
    pool_v = cache_v.transpose(0, 1, 3, 4, 2)
    pool_ki = cache_idx_k.transpose(0, 1, 3, 2)

    nb_all = bp + bs
    nb_pad = -(-nb_all // 8) * 8
    c_all = jnp.pad(jnp.concatenate([c_prompt, c_sample], 0), ((0, nb_pad - nb_all), (0, 0)))
    mod = _ada_mod(c_all, w_ada, b_ada)
    mod_p = mod[:, :, :bp].reshape(depth, 6, bp, 1, d)
    mod_s = mod[:, :, bp:nb_all].reshape(depth, 6, bs, 1, d)

    tt_p = _pick_tile(tp, 512)
    tabs_p = [a.reshape(tp // tt_p, tt_p, LANE) for a in _rope_tables(jnp.arange(tp, dtype=I32))]
    bb_s = _pick_tile(bs, 32)
    tabs_s = [jnp.tile(a, (bb_s, 1)).reshape(1, bb_s * ts, LANE)
              for a in _rope_tables(past + jnp.arange(ts, dtype=I32))]

    tt_lru = _pick_tile(tp, 256)
    tq = _pick_tile(tp, Q_BLOCK)
    n_ranges = max(r for r in (1, 2, 4, 8, 16) if (tp // tq) % r == 0)
    tn_p = _pick_tile(tp, 512)
    bb_peer = _pick_tile(bs, max(1, 512 // ts))

    yp, ys = x_prompt, x_sample
    outs = {k: [] for k in ("kp", "vp", "kip", "cp", "hp", "ks", "vs", "kis", "cs", "hs")}
    for l in range(depth):
        lru_w = (conv_w[l], conv_b[l][None], wa_hi[l], wa_lo[l], lru_ba[l][None], wx_hi[l], wx_lo[l],
                 lru_bx[l][None], lru_lambda[l][None], beta_lru[l][None])
        zx, zg, q, k, v, qi, tail, ki, ki3, k16, v16 = _in_proj(yp, mod_p, l, w_in_hi, w_in_lo,
                                                                tabs_p, 1, tt_p)
        zc = jnp.zeros((bp, CONV_W - 1, lw), F32)
        zh = jnp.zeros((bp, lw), F32)
        mix_lru, nconv, nh = _lru_prompt(zx, zg, zc, zh, lru_w, bp, tp, tt_lru)
        o = _attn_prompt(qi, tail, ki3, q, k16, v16, bp, tp, tq, n_ranges)
        x1 = _out_proj(yp, mod_p, l, mix_lru, o, beta_attn[l][None], w_out_hi, w_out_lo,
                       ln1_g[l][None], ln1_b[l][None], 1, tt_p)
        yp = _peer(x1, mod_p, l, wq_hi, wq_lo, pk3, u_bf, v_bf,
                   ln2_g[l][None], ln2_b[l][None], 1, tn_p, eb)
        outs["kp"].append(k.reshape(bp, tp, N_HEADS, HEAD_DIM))
        outs["vp"].append(v.reshape(bp, tp, N_HEADS, HEAD_DIM))
        outs["kip"].append(ki.reshape(bp, tp, IDX_DIM))
        outs["cp"].append(nconv)
        outs["hp"].append(nh.reshape(bp, lw))
        zx, zg, q, k, v, qi, tail, ki = _in_proj(ys, mod_s, l, w_in_hi, w_in_lo, tabs_s, bb_s, ts)[:8]
        tm = lambda a: a.reshape(bs, ts, lw).transpose(1, 0, 2)
        mix_t, nconv_t, nh = _lru_sample(tm(zx), tm(zg), state_conv[l].transpose(1, 0, 2),
                                         state_lru[l], lru_w)
        mix_lru = mix_t.transpose(1, 0, 2).reshape(bs * ts, lw)
        o = _attn_sample(page_table, qi, tail, q, k, v, pool_ki, pool_k, pool_v, l, bs, ts)
        x1 = _out_proj(ys, mod_s, l, mix_lru, o, beta_attn[l][None], w_out_hi, w_out_lo,
                       ln1_g[l][None], ln1_b[l][None], bb_s, ts)
        ys = _peer(x1, mod_s, l, wq_hi, wq_lo, pk3, u_bf, v_bf,
                   ln2_g[l][None], ln2_b[l][None], bb_peer, ts, eb)
        outs["ks"].append(k.reshape(bs, ts, N_HEADS, HEAD_DIM))
        outs["vs"].append(v.reshape(bs, ts, N_HEADS, HEAD_DIM))
        outs["kis"].append(ki.reshape(bs, ts, IDX_DIM))
        outs["cs"].append(nconv_t.transpose(1, 0, 2))
        outs["hs"].append(nh)

    st = lambda name: jnp.stack(outs[name])
    return (yp, ys, st("kp"), st("vp"), st("kip"), st("cp"), st("hp"),
            st("ks"), st("vs"), st("kis"), st("cs"), st("hs"))
```
